```python
import jax, jax.numpy as jnp
from jax import lax
import numpy as np

D_MODEL = 1024
BATCH = 8
SEQ = 2048
DEPTH = 4
DEC_BATCH = 128
DEC_SEQ = 8
PAST_LEN = 16384
PAGE_SIZE = 128

N_META = 16
GLA_HEADS = 4
GLA_DK = D_MODEL // 2 // GLA_HEADS
GLA_DV = D_MODEL // GLA_HEADS
GLA_KEY = GLA_HEADS * GLA_DK
GLA_VAL = GLA_HEADS * GLA_DV
GATE_RANK = 16
GATE_TAU = 16.0
CHUNK = 64
CONV_CH = D_MODEL
CONV_WIDTH = 31
D_FF = 2816
EPS = 1e-6
SPLITS = (GLA_KEY, GLA_KEY, GLA_VAL, GLA_VAL, GATE_RANK, 2 * CONV_CH, D_MODEL, D_MODEL)
D_IN = GLA_KEY * 2 + GLA_VAL * 2 + GATE_RANK + 2 * CONV_CH + 2 * D_MODEL

kernel_name = "gla_conformer_conv_macaron_decode_step"


def rmsnorm(x, g):
    xf = x.astype(jnp.float32)
    y = xf * lax.rsqrt(jnp.mean(xf * xf, axis=-1, keepdims=True) + EPS)
    return y.astype(x.dtype) * g


def swiglu(x, w_gate, w_up, w_down):
    return (jax.nn.silu(x @ w_gate) * (x @ w_up)) @ w_down


def gla_segment(q, k, v, loga, S0, chunk):
    B, L, H, dk = q.shape
    dv = v.shape[-1]
    n = L // chunk
    q = q.reshape(B, n, chunk, H, dk)
    k = k.reshape(B, n, chunk, H, dk)
    v = v.reshape(B, n, chunk, H, dv)
    loga = loga.reshape(B, n, chunk, H, dk)
    b = jnp.cumsum(loga, axis=2)
    b_last = b[:, :, -1:]
    q_t = q * jnp.exp(b)
    k_t = k * jnp.exp(-b)
    scores = jnp.einsum('bnchd,bnshd->bnhcs', q_t, k_t)
    mask = jnp.tril(jnp.ones((chunk, chunk), dtype=bool))
    scores = jnp.where(mask, scores, 0.0)
    o_intra = jnp.einsum('bnhcs,bnshv->bnchv', scores, v)
    decay = jnp.exp(b_last[:, :, 0])
    upd = jnp.einsum('bnshd,bnshv->bnhdv', k * jnp.exp(b_last - b), v)

    def step(S, xs):
        dec, u = xs
        return dec[..., None] * S + u, S

    S_fin, S_start = lax.scan(step, S0, (jnp.moveaxis(decay, 1, 0), jnp.moveaxis(upd, 1, 0)))
    S_start = jnp.moveaxis(S_start, 0, 1)
    o_inter = jnp.einsum('bnchd,bnhdv->bnchv', q_t, S_start)
    return (o_intra + o_inter).reshape(B, L, H, dv), S_fin


def mixer(x, S0, conv_buf, segments, w_in, w_decay_up, b_decay, gla_norm, w_dw, b_dw, conv_norm, w_pw, w_out):
    B, L, _ = x.shape
    proj = x @ w_in
    cuts = np.cumsum(SPLITS)[:-1].tolist()
    q, k, v, g, a_lr, glu, m_a, m_b = jnp.split(proj, cuts, axis=-1)
    f32 = jnp.float32
    q = q.astype(f32).reshape(B, L, GLA_HEADS, GLA_DK) * (GLA_DK ** -0.5)
    k = k.astype(f32).reshape(B, L, GLA_HEADS, GLA_DK)
    v = v.astype(f32).reshape(B, L, GLA_HEADS, GLA_DV)
    loga = (jax.nn.log_sigmoid((a_lr @ w_decay_up + b_decay).astype(f32)) / GATE_TAU).reshape(B, L, GLA_HEADS, GLA_DK)
    S = S0.astype(f32)
    outs = []
    start = 0
    for length, chunk in segments:
        sl = slice(start, start + length)
        o, S = gla_segment(q[:, sl], k[:, sl], v[:, sl], loga[:, sl], S, chunk)
        outs.append(o)
        start += length
    o = jnp.concatenate(outs, axis=1) if len(outs) > 1 else outs[0]
    o = o * lax.rsqrt(jnp.mean(o * o, axis=-1, keepdims=True) + EPS)
    o_gla = o.reshape(B, L, GLA_VAL).astype(x.dtype) * gla_norm * jax.nn.silu(g)
    a, ga = jnp.split(glu, 2, axis=-1)
    u = a * jax.nn.sigmoid(ga)
    up = jnp.concatenate([conv_buf.astype(u.dtype), u], axis=1)
    new_buf = up[:, -(CONV_WIDTH - 1):]
    c = lax.conv_general_dilated(up, w_dw[:, None, :].astype(up.dtype), window_strides=(1,), padding='VALID',
                                 dimension_numbers=('NWC', 'WIO', 'NWC'), feature_group_count=CONV_CH) + b_dw
    c = jax.nn.silu(rmsnorm(c, conv_norm))
    o_conv = c @ w_pw
    merged = jax.nn.sigmoid(m_a) * o_gla + jax.nn.sigmoid(m_b) * o_conv
    return merged @ w_out, S.astype(x.dtype), new_buf


def trunk(h, S_list, buf_list, segments, norm_ffn1, w_ffn1_gate, w_ffn1_up, w_ffn1_down, norm_mix, w_in,
          w_decay_up, b_decay, gla_norm, w_dw, b_dw, conv_norm, w_pw, w_out, norm_ffn2, w_ffn2_gate,
          w_ffn2_up, w_ffn2_down, norm_final):
    new_S, new_buf = [], []
    for l in range(DEPTH):
        h = h + 0.5 * swiglu(rmsnorm(h, norm_ffn1[l]), w_ffn1_gate[l], w_ffn1_up[l], w_ffn1_down[l])
        m, S, buf = mixer(rmsnorm(h, norm_mix[l]), S_list[l], buf_list[l], segments, w_in[l], w_decay_up[l],
                          b_decay[l], gla_norm[l], w_dw[l], b_dw[l], conv_norm[l], w_pw[l], w_out[l])
        h = h + m
        h = h + 0.5 * swiglu(rmsnorm(h, norm_ffn2[l]), w_ffn2_gate[l], w_ffn2_up[l], w_ffn2_down[l])
        new_S.append(S)
        new_buf.append(buf)
    return rmsnorm(h, norm_final), jnp.stack(new_S), jnp.stack(new_buf)


def setup_inputs(seed: int = 0) -> dict:
    key = jax.random.key(seed)
    ks = iter(jax.random.split(key, 32))
    f32 = jnp.float32

    def w(shape, fan_in):
        return jax.random.normal(next(ks), shape, f32) * (fan_in ** -0.5)

    def gain(shape):
        return jnp.ones(shape, f32) + 0.01 * jax.random.normal(next(ks), shape, f32)

    def small(shape, s):
        return s * jax.random.normal(next(ks), shape, f32)

    return {
        "x_prompt": jax.random.normal(next(ks), (BATCH, SEQ, D_MODEL), f32),
        "x_sample": jax.random.normal(next(ks), (DEC_BATCH, DEC_SEQ, D_MODEL), f32),
        "state_gla": jax.random.normal(next(ks), (DEPTH, DEC_BATCH, GLA_HEADS, GLA_DK, GLA_DV), f32),
        "cache_conv": 0.5 * jax.random.normal(next(ks), (DEPTH, DEC_BATCH, CONV_WIDTH - 1, CONV_CH), f32),
        "meta_tokens": jax.random.normal(next(ks), (N_META, D_MODEL), f32),
        "norm_ffn1": gain((DEPTH, D_MODEL)),
        "w_ffn1_gate": w((DEPTH, D_MODEL, D_FF), D_MODEL),
        "w_ffn1_up": w((DEPTH, D_MODEL, D_FF), D_MODEL),
        "w_ffn1_down": w((DEPTH, D_FF, D_MODEL), D_FF),
        "norm_mix": gain((DEPTH, D_MODEL)),
        "w_in": w((DEPTH, D_MODEL, D_IN), D_MODEL),
        "w_decay_up": w((DEPTH, GATE_RANK, GLA_KEY), GATE_RANK),
        "b_decay": small((DEPTH, GLA_KEY), 0.1),
        "gla_norm": gain((DEPTH, GLA_VAL)),
        "w_dw": w((DEPTH, CONV_WIDTH, CONV_CH), CONV_WIDTH),
        "b_dw": small((DEPTH, CONV_CH), 0.02),
        "conv_norm": gain((DEPTH, CONV_CH)),
        "w_pw": w((DEPTH, CONV_CH, D_MODEL), CONV_CH),
        "w_out": w((DEPTH, D_MODEL, D_MODEL), D_MODEL),
        "norm_ffn2": gain((DEPTH, D_MODEL)),
        "w_ffn2_gate": w((DEPTH, D_MODEL, D_FF), D_MODEL),
        "w_ffn2_up": w((DEPTH, D_MODEL, D_FF), D_MODEL),
        "w_ffn2_down": w((DEPTH, D_FF, D_MODEL), D_FF),
        "norm_final": gain((D_MODEL,)),
    }


def reference(x_prompt, x_sample, state_gla, cache_conv, meta_tokens, norm_ffn1, w_ffn1_gate, w_ffn1_up,
              w_ffn1_down, norm_mix, w_in, w_decay_up, b_decay, gla_norm, w_dw, b_dw, conv_norm, w_pw, w_out,
              norm_ffn2, w_ffn2_gate, w_ffn2_up, w_ffn2_down, norm_final):
    weights = (norm_ffn1, w_ffn1_gate, w_ffn1_up, w_ffn1_down, norm_mix, w_in, w_decay_up, b_decay, gla_norm,
               w_dw, b_dw, conv_norm, w_pw, w_out, norm_ffn2, w_ffn2_gate, w_ffn2_up, w_ffn2_down, norm_final)
    B = x_prompt.shape[0]
    dt = x_prompt.dtype
    meta = jnp.broadcast_to(meta_tokens[None].astype(dt), (B, N_META, D_MODEL))
    h_p = jnp.concatenate([meta, x_prompt], axis=1)
    S0_p = [jnp.zeros((B, GLA_HEADS, GLA_DK, GLA_DV), dt) for _ in range(DEPTH)]
    buf0_p = [jnp.zeros((B, CONV_WIDTH - 1, CONV_CH), dt) for _ in range(DEPTH)]
    seg_p = ((N_META, N_META), (x_prompt.shape[1], min(CHUNK, x_prompt.shape[1])))
    out_p, state_gla_prompt, cache_conv_prompt = trunk(h_p, S0_p, buf0_p, seg_p, *weights)
    y_prompt = out_p[:, N_META:]
    L_s = x_sample.shape[1]
    S0_s = [state_gla[l] for l in range(DEPTH)]
    buf0_s = [cache_conv[l] for l in range(DEPTH)]
    y_sample, state_gla_sample, cache_conv_sample = trunk(x_sample, S0_s, buf0_s, ((L_s, L_s),), *weights)
    return (y_prompt, y_sample, state_gla_prompt, cache_conv_prompt, state_gla_sample, cache_conv_sample)
```

```python
import functools

import jax
import jax.numpy as jnp
from jax import lax
from jax.experimental import pallas as pl
from jax.experimental.pallas import tpu as pltpu

F32 = jnp.float32
BF16 = jnp.bfloat16

D_MODEL = 1024
DEPTH = 4
N_META = 16
GLA_HEADS = 4
GLA_DK = 128
GLA_DV = 256
GLA_KEY = GLA_HEADS * GLA_DK
GLA_VAL = GLA_HEADS * GLA_DV
GATE_RANK = 16
GATE_TAU = 16.0
CHUNK = 64
CONV_WIDTH = 31
CONV_HIST = CONV_WIDTH - 1
D_FF = 2816
EPS = 1e-6

LANE = 128
FF_COLS = 256
HIST_ROWS = 32
VMEM_LIMIT = 56 * 1024 * 1024


def _cparams(n_axes):
    return pltpu.CompilerParams(dimension_semantics=("arbitrary",) * n_axes, vmem_limit_bytes=VMEM_LIMIT)


def _const_spec(shape):
    nd = len(shape)
    return pl.BlockSpec(shape, lambda *_: (0,) * nd, pipeline_mode=pl.Buffered(1))


def _rmsnorm(x, g):
    return x * lax.rsqrt(jnp.mean(x * x, axis=-1, keepdims=True) + EPS) * g


def _sigmoid(x):
    return 1.0 / (1.0 + jnp.exp(-x))


def _silu(x):
    return x * _sigmoid(x)


def _log_sigmoid(x):
    return jnp.minimum(x, 0.0) - jnp.log1p(jnp.exp(-jnp.abs(x)))


def _dot(a, b):
    return jnp.dot(a, b, preferred_element_type=F32)


def _ffn_body(h_ref, g_ref, wg_ref, wu_ref, wd_ref, o_ref):
    x = h_ref[...]
    xn = _rmsnorm(x, g_ref[...]).astype(BF16)
    acc = None
    for c in range(D_FF // FF_COLS):
        cols = slice(c * FF_COLS, (c + 1) * FF_COLS)
        gate = _dot(xn, wg_ref[:, cols])
        up = _dot(xn, wu_ref[:, cols])
        act = (_silu(gate) * up).astype(BF16)
        part = _dot(act, wd_ref[cols, :])
        acc = part if acc is None else acc + part
    o_ref[...] = x + 0.5 * acc


def _ffn(h, g, wg, wu, wd, tm):
    rows = h.shape[0]
    row_spec = pl.BlockSpec((tm, D_MODEL), lambda i: (i, 0))
    return pl.pallas_call(
        _ffn_body,
        out_shape=jax.ShapeDtypeStruct((rows, D_MODEL), F32),
        grid=(rows // tm,),
        in_specs=[row_spec, _const_spec((1, D_MODEL)), _const_spec((D_MODEL, D_FF)),
                  _const_spec((D_MODEL, D_FF)), _const_spec((D_FF, D_MODEL))],
        out_specs=row_spec,
        compiler_params=_cparams(1),
        name="ffn",
    )(h, g, wg, wu, wd)


_OFF_Q, _OFF_K, _OFF_V, _OFF_G, _OFF_A, _OFF_GA, _OFF_MA, _OFF_MB, _OFF_END = (
    0, 512, 1024, 2048, 3072, 4096, 5120, 6144, 7168)


def _proj_body(h_ref, g_ref, wm_ref, wlr_ref, wdu_ref, bdec_ref,
               q_ref, k_ref, la_ref, v_ref, gs_ref, u_ref, sa_ref, sb_ref):
    xn = _rmsnorm(h_ref[...], g_ref[...]).astype(BF16)

    def mm(lo, hi):
        return _dot(xn, wm_ref[:, lo:hi])

    q_ref[...] = mm(_OFF_Q, _OFF_K) * (GLA_DK ** -0.5)
    k_ref[...] = mm(_OFF_K, _OFF_V)
    v_ref[...] = mm(_OFF_V, _OFF_G)
    gs_ref[...] = _silu(mm(_OFF_G, _OFF_A))
    u_ref[...] = mm(_OFF_A, _OFF_GA) * _sigmoid(mm(_OFF_GA, _OFF_MA))
    sa_ref[...] = _sigmoid(mm(_OFF_MA, _OFF_MB))
    sb_ref[...] = _sigmoid(mm(_OFF_MB, _OFF_END))
    a_lr = _dot(xn, wlr_ref[...]).astype(BF16)
    z = _dot(a_lr, wdu_ref[...]) + bdec_ref[...]
    la_ref[...] = _log_sigmoid(z) / GATE_TAU


def _proj(h, g, wm, wlr, wdu, bdec, tm):
    rows = h.shape[0]

    def rs(n):
        return pl.BlockSpec((tm, n), lambda i: (i, 0))

    widths = (GLA_KEY, GLA_KEY, GLA_KEY, GLA_VAL, GLA_VAL, D_MODEL, D_MODEL, D_MODEL)
    return pl.pallas_call(
        _proj_body,
        out_shape=[jax.ShapeDtypeStruct((rows, n), F32) for n in widths],
        grid=(rows // tm,),
        in_specs=[rs(D_MODEL), _const_spec((1, D_MODEL)), _const_spec((D_MODEL, _OFF_END)),
                  _const_spec((D_MODEL, LANE)), _const_spec((LANE, GLA_KEY)), _const_spec((1, GLA_KEY))],
        out_specs=[rs(n) for n in widths],
        compiler_params=_cparams(1),
        name="proj",
    )(h, g, wm, wlr, wdu, bdec)


def _chunk_cumsum(x, chunk):
    pos = lax.broadcasted_iota(jnp.int32, x.shape, 0) % chunk
    shift = 1
    while shift < chunk:
        x = x + jnp.where(pos >= shift, pltpu.roll(x, shift, 0), 0.0)
        shift *= 2
    return x


def _gla_chunk(q, k, v, b, state):
    c = q.shape[0]
    b_last = b[c - 1:c, :]
    q_t = (q * jnp.exp(b)).astype(BF16)
    k_t = (k * jnp.exp(-b)).astype(BF16)
    k_u = (k * jnp.exp(b_last - b)).astype(BF16)
    v_b = v.astype(BF16)
    scores = lax.dot_general(q_t, k_t, (((1,), (1,)), ((), ())), preferred_element_type=F32)
    row = lax.broadcasted_iota(jnp.int32, (c, c), 0)
    col = lax.broadcasted_iota(jnp.int32, (c, c), 1)
    scores = jnp.where(row >= col, scores, 0.0).astype(BF16)
    out = _dot(scores, v_b) + _dot(q_t, state.astype(BF16))
    upd = lax.dot_general(k_u, v_b, (((0,), (0,)), ((), ())), preferred_element_type=F32)
    r2 = lax.broadcasted_iota(jnp.int32, (GLA_DK, GLA_DK), 0)
    c2 = lax.broadcasted_iota(jnp.int32, (GLA_DK, GLA_DK), 1)
    dec_row = jnp.broadcast_to(jnp.exp(b_last), (GLA_DK, GLA_DK))
    dec_col = jnp.sum(jnp.where(r2 == c2, dec_row, 0.0), axis=1, keepdims=True)
    return out, dec_col * state + upd


def _head_norm_gate(o, gnorm, gs):
    return o * lax.rsqrt(jnp.mean(o * o, axis=-1, keepdims=True) + EPS) * gnorm * gs


def _gla_seq_body(q_ref, k_ref, la_ref, v_ref, gs_ref, s0_ref, gn_ref, o_ref, sfin_ref, s_scr, *, tl, chunk):
    t = pl.program_id(1)

    @pl.when(t == 0)
    def _():
        s_scr[...] = s0_ref[...]

    b_all = _chunk_cumsum(la_ref[...], chunk)
    for h in range(GLA_HEADS):
        kc = slice(h * GLA_DK, (h + 1) * GLA_DK)
        vc = slice(h * GLA_DV, (h + 1) * GLA_DV)
        state = s_scr[h]
        for c in range(tl // chunk):
            rows = slice(c * chunk, (c + 1) * chunk)
            o, state = _gla_chunk(q_ref[rows, kc], k_ref[rows, kc], v_ref[rows, vc], b_all[rows, kc], state)
            o_ref[rows, vc] = _head_norm_gate(o, gn_ref[:, vc], gs_ref[rows, vc])
        s_scr[h] = state

    @pl.when(t == pl.num_programs(1) - 1)
    def _():
        sfin_ref[...] = s_scr[...]


def _gla_seq(q, k, la, v, gs, s0, gnorm, tl, chunk):
    bsz, seq, _ = q.shape

    def ts(n):
        return pl.BlockSpec((None, tl, n), lambda b, t: (b, t, 0))

    state_shape = (GLA_HEADS, GLA_DK, GLA_DV)
    return pl.pallas_call(
        functools.partial(_gla_seq_body, tl=tl, chunk=chunk),
        out_shape=[jax.ShapeDtypeStruct((bsz, seq, GLA_VAL), F32),
                   jax.ShapeDtypeStruct((bsz,) + state_shape, F32)],
        grid=(bsz, seq // tl),
        in_specs=[ts(GLA_KEY), ts(GLA_KEY), ts(GLA_KEY), ts(GLA_VAL), ts(GLA_VAL),
                  _const_spec(state_shape), _const_spec((1, GLA_VAL))],
        out_specs=[ts(GLA_VAL), pl.BlockSpec((None,) + state_shape, lambda b, t: (b, 0, 0, 0))],
        scratch_shapes=[pltpu.VMEM(state_shape, F32)],
        compiler_params=_cparams(2),
        name="gla_seq",
    )(q, k, la, v, gs, s0, gnorm)


def _gla_step_body(q_ref, k_ref, la_ref, v_ref, gs_ref, s0_ref, gn_ref, o_ref, sfin_ref, *, sb, seq):
    for i in range(sb):
        b_all = _chunk_cumsum(la_ref[i], seq)
        for h in range(GLA_HEADS):
            kc = slice(h * GLA_DK, (h + 1) * GLA_DK)
            vc = slice(h * GLA_DV, (h + 1) * GLA_DV)
            o, state = _gla_chunk(q_ref[i, :, kc], k_ref[i, :, kc], v_ref[i, :, vc], b_all[:, kc], s0_ref[i, h])
            o_ref[i, :, vc] = _head_norm_gate(o, gn_ref[:, vc], gs_ref[i, :, vc])
            sfin_ref[i, h] = state


def _gla_step(q, k, la, v, gs, s0, gnorm, sb):
    bsz, seq, _ = q.shape

    def bs(n):
        return pl.BlockSpec((sb, seq, n), lambda i: (i, 0, 0))

    st = pl.BlockSpec((sb, GLA_HEADS, GLA_DK, GLA_DV), lambda i: (i, 0, 0, 0))
    return pl.pallas_call(
        functools.partial(_gla_step_body, sb=sb, seq=seq),
        out_shape=[jax.ShapeDtypeStruct((bsz, seq, GLA_VAL), F32),
                   jax.ShapeDtypeStruct((bsz, GLA_HEADS, GLA_DK, GLA_DV), F32)],
        grid=(bsz // sb,),
        in_specs=[bs(GLA_KEY), bs(GLA_KEY), bs(GLA_KEY), bs(GLA_VAL), bs(GLA_VAL), st, _const_spec((1, GLA_VAL))],
        out_specs=[bs(GLA_VAL), st],
        compiler_params=_cparams(1),
        name="gla_step",
    )(q, k, la, v, gs, s0, gnorm)


def _conv_body(u_ref, cache_ref, wdw_ref, bdw_ref, cn_ref, c_ref, ncache_ref, ubuf, *, sb, tl, rb):
    t = pl.program_id(1)
    pad = HIST_ROWS - CONV_HIST

    @pl.when(t == 0)
    def _():
        for i in range(sb):
            ubuf[i, 0:pad, :] = jnp.zeros((pad, D_MODEL), F32)
            ubuf[i, pad:HIST_ROWS, :] = cache_ref[i]

    @pl.when(t > 0)
    def _():
        for i in range(sb):
            ubuf[i, 0:HIST_ROWS, :] = ubuf[i, tl:tl + HIST_ROWS, :]

    for i in range(sb):
        ubuf[i, HIST_ROWS:HIST_ROWS + tl, :] = u_ref[i]
        for r in range(tl // rb):
            acc = jnp.broadcast_to(bdw_ref[...], (rb, D_MODEL))
            for j in range(CONV_WIDTH):
                acc = acc + wdw_ref[j:j + 1, :] * ubuf[i, r * rb + j + pad:r * rb + j + pad + rb, :]
            c_ref[i, r * rb:(r + 1) * rb, :] = _silu(_rmsnorm(acc, cn_ref[...]))

    @pl.when(t == pl.num_programs(1) - 1)
    def _():
        for i in range(sb):
            ncache_ref[i] = ubuf[i, tl + pad:tl + HIST_ROWS, :]


def _conv(u, cache, wdw, bdw, cnorm, sb, tl, shared_cache):
    bsz, seq, _ = u.shape
    rb = min(tl, 64)
    cache_idx = (lambda b, t: (0, 0, 0)) if shared_cache else (lambda b, t: (b, 0, 0))
    return pl.pallas_call(
        functools.partial(_conv_body, sb=sb, tl=tl, rb=rb),
        out_shape=[jax.ShapeDtypeStruct((bsz, seq, D_MODEL), F32),
                   jax.ShapeDtypeStruct((bsz, CONV_HIST, D_MODEL), F32)],
        grid=(bsz // sb, seq // tl),
        in_specs=[pl.BlockSpec((sb, tl, D_MODEL), lambda b, t: (b, t, 0)),
                  pl.BlockSpec((sb, CONV_HIST, D_MODEL), cache_idx),
                  _const_spec((CONV_WIDTH, D_MODEL)), _const_spec((1, D_MODEL)), _const_spec((1, D_MODEL))],
        out_specs=[pl.BlockSpec((sb, tl, D_MODEL), lambda b, t: (b, t, 0)),
                   pl.BlockSpec((sb, CONV_HIST, D_MODEL), lambda b, t: (b, 0, 0))],
        scratch_shapes=[pltpu.VMEM((sb, tl + HIST_ROWS, D_MODEL), F32)],
        compiler_params=_cparams(2),
        name="conv",
    )(u, cache, wdw, bdw, cnorm)


def _merge_body(h_ref, c_ref, og_ref, sa_ref, sb_ref, wpw_ref, wout_ref, o_ref):
    o_conv = _dot(c_ref[...].astype(BF16), wpw_ref[...])
    merged = (sa_ref[...] * og_ref[...] + sb_ref[...] * o_conv).astype(BF16)
    o_ref[...] = h_ref[...] + _dot(merged, wout_ref[...])


def _merge(h, c, og, sa, sb, wpw, wout, tm):
    rows = h.shape[0]
    row_spec = pl.BlockSpec((tm, D_MODEL), lambda i: (i, 0))
    return pl.pallas_call(
        _merge_body,
        out_shape=jax.ShapeDtypeStruct((rows, D_MODEL), F32),
        grid=(rows // tm,),
        in_specs=[row_spec] * 5 + [_const_spec((D_MODEL, D_MODEL))] * 2,
        out_specs=row_spec,
        compiler_params=_cparams(1),
        name="merge",
    )(h, c, og, sa, sb, wpw, wout)


def _final_norm_body(h_ref, g_ref, o_ref):
    o_ref[...] = _rmsnorm(h_ref[...], g_ref[...])


def _final_norm(h, g, tm):
    rows = h.shape[0]
    row_spec = pl.BlockSpec((tm, D_MODEL), lambda i: (i, 0))
    return pl.pallas_call(
        _final_norm_body,
        out_shape=jax.ShapeDtypeStruct((rows, D_MODEL), F32),
        grid=(rows // tm,),
        in_specs=[row_spec, _const_spec((1, D_MODEL))],
        out_specs=row_spec,
        compiler_params=_cparams(1),
        name="final_norm",
    )(h, g)


def kernel(x_prompt, x_sample, state_gla, cache_conv, meta_tokens, norm_ffn1, w_ffn1_gate, w_ffn1_up, w_ffn1_down, norm_mix, w_in, w_decay_up, b_decay, gla_norm, w_dw, b_dw, conv_norm, w_pw, w_out, norm_ffn2, w_ffn2_gate, w_ffn2_up, w_ffn2_down, norm_final):
    n_prompt, l_prompt, _ = x_prompt.shape
    n_sample, l_sample, _ = x_sample.shape
    rows_p = n_prompt * l_prompt
    rows_s = n_sample * l_sample
    rows_small = rows_s + N_META

    hp = x_prompt.reshape(rows_p, D_MODEL)
    hs = jnp.concatenate([x_sample.reshape(rows_s, D_MODEL), meta_tokens], axis=0)

    tm_ffn = (512, rows_small)
    tm_proj = (256, 208)
    tm_merge = (512, rows_small)
    tl_prompt = 256

    lr0 = 2 * GLA_KEY + 2 * GLA_VAL
    w_main = jnp.concatenate([w_in[:, :, :lr0], w_in[:, :, lr0 + GATE_RANK:]], axis=2).astype(BF16)
    w_lr = jnp.pad(w_in[:, :, lr0:lr0 + GATE_RANK], ((0, 0), (0, 0), (0, LANE - GATE_RANK))).astype(BF16)
    w_du = jnp.pad(w_decay_up, ((0, 0), (0, LANE - GATE_RANK), (0, 0))).astype(BF16)

    zero_state = jnp.zeros((1, GLA_HEADS, GLA_DK, GLA_DV), F32)
    zero_cache = jnp.zeros((1, CONV_HIST, D_MODEL), F32)

    sp_list, cp_list, ss_list, cs_list = [], [], [], []
    for l in range(DEPTH):
        def row(a):
            return a[l].reshape(1, -1)

        ffn1_w = (row(norm_ffn1), w_ffn1_gate[l].astype(BF16), w_ffn1_up[l].astype(BF16), w_ffn1_down[l].astype(BF16))
        ffn2_w = (row(norm_ffn2), w_ffn2_gate[l].astype(BF16), w_ffn2_up[l].astype(BF16), w_ffn2_down[l].astype(BF16))
        proj_w = (row(norm_mix), w_main[l], w_lr[l], w_du[l], row(b_decay))
        conv_w = (w_dw[l], row(b_dw), row(conv_norm))
        merge_w = (w_pw[l].astype(BF16), w_out[l].astype(BF16))
        gnorm = row(gla_norm)

        hp = _ffn(hp, *ffn1_w, tm_ffn[0])
        hs = _ffn(hs, *ffn1_w, tm_ffn[1])

        q, k, la, v, gs, u, sa, sb = _proj(hs, *proj_w, tm_proj[1])

        def seqs(a, lo, n, length):
            return a[lo:lo + n * length].reshape(n, length, a.shape[-1])

        samp = [seqs(a, 0, n_sample, l_sample) for a in (q, k, la, v, gs)]
        meta = [seqs(a, rows_s, 1, N_META) for a in (q, k, la, v, gs)]
        og_s, s_s = _gla_step(*samp, state_gla[l], gnorm, 8)
        og_m, s_m = _gla_step(*meta, zero_state, gnorm, 1)
        c_s, nc_s = _conv(seqs(u, 0, n_sample, l_sample), cache_conv[l], *conv_w, 16, l_sample, False)
        c_m, nc_m = _conv(seqs(u, rows_s, 1, N_META), zero_cache, *conv_w, 1, N_META, False)
        og = jnp.concatenate([og_s.reshape(rows_s, GLA_VAL), og_m.reshape(N_META, GLA_VAL)], axis=0)
        cc = jnp.concatenate([c_s.reshape(rows_s, D_MODEL), c_m.reshape(N_META, D_MODEL)], axis=0)
        hs = _merge(hs, cc, og, sa, sb, *merge_w, tm_merge[1])
        ss_list.append(s_s)
        cs_list.append(nc_s)

        q, k, la, v, gs, u, sa, sb = _proj(hp, *proj_w, tm_proj[0])

        def pseq(a):
            return a.reshape(n_prompt, l_prompt, a.shape[-1])

        og_p, s_p = _gla_seq(pseq(q), pseq(k), pseq(la), pseq(v), pseq(gs), s_m[0], gnorm, tl_prompt, CHUNK)
        c_p, nc_p = _conv(pseq(u), nc_m, *conv_w, 1, tl_prompt, True)
        hp = _merge(hp, c_p.reshape(rows_p, D_MODEL), og_p.reshape(rows_p, GLA_VAL), sa, sb, *merge_w, tm_merge[0])
        sp_list.append(s_p)
        cp_list.append(nc_p)

        hp = _ffn(hp, *ffn2_w, tm_ffn[0])
        hs = _ffn(hs, *ffn2_w, tm_ffn[1])

    gfin = norm_final.reshape(1, D_MODEL)
    y_prompt = _final_norm(hp, gfin, 512).reshape(n_prompt, l_prompt, D_MODEL)
    y_sample = _final_norm(hs, gfin, rows_small)[:rows_s].reshape(n_sample, l_sample, D_MODEL)
    return (y_prompt, y_sample, jnp.stack(sp_list), jnp.stack(cp_list), jnp.stack(ss_list), jnp.stack(cs_list))
```

```python
import functools

import jax
import jax.numpy as jnp
from jax import lax
from jax.experimental import pallas as pl
from jax.experimental.pallas import tpu as pltpu

F32 = jnp.float32
BF16 = jnp.bfloat16

D_MODEL = 1024
DEPTH = 4
N_META = 16
GLA_HEADS = 4
GLA_DK = 128
GLA_DV = 256
GLA_KEY = GLA_HEADS * GLA_DK
GLA_VAL = GLA_HEADS * GLA_DV
GATE_RANK = 16
GATE_TAU = 16.0
CHUNK = 64
CONV_WIDTH = 31
CONV_HIST = CONV_WIDTH - 1
D_FF = 2816
EPS = 1e-6

LANE = 128
FF_COLS = 256
HIST_ROWS = 32
CONV_RB = 64
VMEM_LIMIT = 56 * 1024 * 1024


def _cparams(n_axes):
    return pltpu.CompilerParams(dimension_semantics=("arbitrary",) * n_axes, vmem_limit_bytes=VMEM_LIMIT)


def _const_spec(shape):
    nd = len(shape)
    return pl.BlockSpec(shape, lambda *_: (0,) * nd, pipeline_mode=pl.Buffered(1))


def _rmsnorm(x, g):
    return x * lax.rsqrt(jnp.mean(x * x, axis=-1, keepdims=True) + EPS) * g


def _sigmoid(x):
    return 1.0 / (1.0 + jnp.exp(-x))


def _silu(x):
    return x * _sigmoid(x)


def _log_sigmoid(x):
    return jnp.minimum(x, 0.0) - jnp.log1p(jnp.exp(-jnp.abs(x)))


def _dot(a, b):
    return jnp.dot(a, b, preferred_element_type=F32)


def _ffn_math(x, g, wg_ref, wu_ref, wd_ref):
    xn = _rmsnorm(x, g).astype(BF16)
    acc = None
    for c in range(D_FF // FF_COLS):
        cols = slice(c * FF_COLS, (c + 1) * FF_COLS)
        gate = _dot(xn, wg_ref[:, cols])
        up = _dot(xn, wu_ref[:, cols])
        act = (_silu(gate) * up).astype(BF16)
        part = _dot(act, wd_ref[cols, :])
        acc = part if acc is None else acc + part
    return x + 0.5 * acc


def _ffn_body(h_ref, g_ref, wg_ref, wu_ref, wd_ref, o_ref):
    o_ref[...] = _ffn_math(h_ref[...], g_ref[...], wg_ref, wu_ref, wd_ref)


def _ffn(h, g, wg, wu, wd, tm):
    rows = h.shape[0]
    row_spec = pl.BlockSpec((tm, D_MODEL), lambda i: (i, 0))
    return pl.pallas_call(
        _ffn_body,
        out_shape=jax.ShapeDtypeStruct((rows, D_MODEL), F32),
        grid=(rows // tm,),
        in_specs=[row_spec, _const_spec((1, D_MODEL)), _const_spec((D_MODEL, D_FF)),
                  _const_spec((D_MODEL, D_FF)), _const_spec((D_FF, D_MODEL))],
        out_specs=row_spec,
        compiler_params=_cparams(1),
        name="ffn",
    )(h, g, wg, wu, wd)


_OFF_Q, _OFF_K, _OFF_V, _OFF_G, _OFF_A, _OFF_GA, _OFF_MA, _OFF_MB, _OFF_END = (
    0, 512, 1024, 2048, 3072, 4096, 5120, 6144, 7168)


def _proj_body(h_ref, g_ref, wm_ref, wlr_ref, wdu_ref, bdec_ref,
               q_ref, k_ref, la_ref, v_ref, gs_ref, u_ref, sa_ref, sb_ref):
    xn = _rmsnorm(h_ref[...], g_ref[...]).astype(BF16)

    def mm(lo, hi):
        return _dot(xn, wm_ref[:, lo:hi])

    q_ref[...] = mm(_OFF_Q, _OFF_K) * (GLA_DK ** -0.5)
    k_ref[...] = mm(_OFF_K, _OFF_V)
    v_ref[...] = mm(_OFF_V, _OFF_G)
    gs_ref[...] = _silu(mm(_OFF_G, _OFF_A))
    u_ref[...] = mm(_OFF_A, _OFF_GA) * _sigmoid(mm(_OFF_GA, _OFF_MA))
    sa_ref[...] = _sigmoid(mm(_OFF_MA, _OFF_MB))
    sb_ref[...] = _sigmoid(mm(_OFF_MB, _OFF_END))
    a_lr = _dot(xn, wlr_ref[...]).astype(BF16)
    z = _dot(a_lr, wdu_ref[...]) + bdec_ref[...]
    la_ref[...] = _log_sigmoid(z) / GATE_TAU


def _proj(h, g, wm, wlr, wdu, bdec, tm):
    rows = h.shape[0]

    def rs(n):
        return pl.BlockSpec((tm, n), lambda i: (i, 0))

    widths = (GLA_KEY, GLA_KEY, GLA_KEY, GLA_VAL, GLA_VAL, D_MODEL, D_MODEL, D_MODEL)
    return pl.pallas_call(
        _proj_body,
        out_shape=[jax.ShapeDtypeStruct((rows, n), F32) for n in widths],
        grid=(rows // tm,),
        in_specs=[rs(D_MODEL), _const_spec((1, D_MODEL)), _const_spec((D_MODEL, _OFF_END)),
                  _const_spec((D_MODEL, LANE)), _const_spec((LANE, GLA_KEY)), _const_spec((1, GLA_KEY))],
        out_specs=[rs(n) for n in widths],
        compiler_params=_cparams(1),
        name="proj",
    )(h, g, wm, wlr, wdu, bdec)


def _chunk_cumsum(x, chunk):
    pos = lax.broadcasted_iota(jnp.int32, x.shape, 0) % chunk
    shift = 1
    while shift < chunk:
        x = x + jnp.where(pos >= shift, pltpu.roll(x, shift, 0), 0.0)
        shift *= 2
    return x


def _gla_chunk(q, k, v, b, state):
    c = q.shape[0]
    b_last = b[c - 1:c, :]
    q_t = (q * jnp.exp(b)).astype(BF16)
    k_t = (k * jnp.exp(-b)).astype(BF16)
    k_u = (k * jnp.exp(b_last - b)).astype(BF16)
    v_b = v.astype(BF16)
    scores = lax.dot_general(q_t, k_t, (((1,), (1,)), ((), ())), preferred_element_type=F32)
    row = lax.broadcasted_iota(jnp.int32, (c, c), 0)
    col = lax.broadcasted_iota(jnp.int32, (c, c), 1)
    scores = jnp.where(row >= col, scores, 0.0).astype(BF16)
    out = _dot(scores, v_b) + _dot(q_t, state.astype(BF16))
    upd = lax.dot_general(k_u, v_b, (((0,), (0,)), ((), ())), preferred_element_type=F32)
    r2 = lax.broadcasted_iota(jnp.int32, (GLA_DK, GLA_DK), 0)
    c2 = lax.broadcasted_iota(jnp.int32, (GLA_DK, GLA_DK), 1)
    dec_row = jnp.broadcast_to(jnp.exp(b_last), (GLA_DK, GLA_DK))
    dec_col = jnp.sum(jnp.where(r2 == c2, dec_row, 0.0), axis=1, keepdims=True)
    return out, dec_col * state + upd


def _head_norm_gate(o, gnorm, gs):
    return o * lax.rsqrt(jnp.mean(o * o, axis=-1, keepdims=True) + EPS) * gnorm * gs


def _gla_seq_body(q_ref, k_ref, la_ref, v_ref, gs_ref, s0_ref, gn_ref, o_ref, sfin_ref, s_scr, *, tl, chunk):
    t = pl.program_id(1)

    @pl.when(t == 0)
    def _():
        s_scr[...] = s0_ref[...]

    n_blk = tl // chunk
    heads = range(GLA_HEADS)
    kcs = [slice(h * GLA_DK, (h + 1) * GLA_DK) for h in heads]
    vcs = [slice(h * GLA_DV, (h + 1) * GLA_DV) for h in heads]
    nt_dims = (((1,), (1,)), ((), ()))
    tn_dims = (((0,), (0,)), ((), ()))
    b_all = _chunk_cumsum(la_ref[...], chunk)
    q_t, k_t, k_u, v_b, dec = [], [], [], [], []
    for c in range(n_blk):
        rows = slice(c * chunk, (c + 1) * chunk)
        b = b_all[rows, :]
        b_last = b[chunk - 1:chunk, :]
        k = k_ref[rows, :]
        q_t.append((q_ref[rows, :] * jnp.exp(b)).astype(BF16))
        k_t.append((k * jnp.exp(-b)).astype(BF16))
        k_u.append((k * jnp.exp(b_last - b)).astype(BF16))
        v_b.append(v_ref[rows, :].astype(BF16))
        dec.append(jnp.exp(b_last))
    tri = (lax.broadcasted_iota(jnp.int32, (chunk, chunk), 0) >= lax.broadcasted_iota(jnp.int32, (chunk, chunk), 1))
    eye = (lax.broadcasted_iota(jnp.int32, (GLA_DK, GLA_DK), 0) == lax.broadcasted_iota(jnp.int32, (GLA_DK, GLA_DK), 1))
    probs = [[jnp.where(tri, lax.dot_general(q_t[c][:, kcs[h]], k_t[c][:, kcs[h]], nt_dims,
                                            preferred_element_type=F32), 0.0).astype(BF16)
              for h in heads] for c in range(n_blk)]
    upd = [[lax.dot_general(k_u[c][:, kcs[h]], v_b[c][:, vcs[h]], tn_dims, preferred_element_type=F32)
            for h in heads] for c in range(n_blk)]
    dec_col = [[jnp.sum(jnp.where(eye, jnp.broadcast_to(dec[c][:, kcs[h]], (GLA_DK, GLA_DK)), 0.0),
                        axis=1, keepdims=True) for h in heads] for c in range(n_blk)]
    states = [s_scr[h] for h in heads]
    for c in range(n_blk):
        rows = slice(c * chunk, (c + 1) * chunk)
        for h in heads:
            o = _dot(probs[c][h], v_b[c][:, vcs[h]]) + _dot(q_t[c][:, kcs[h]], states[h].astype(BF16))
            states[h] = dec_col[c][h] * states[h] + upd[c][h]
            o_ref[rows, vcs[h]] = _head_norm_gate(o, gn_ref[:, vcs[h]], gs_ref[rows, vcs[h]])
    for h in heads:
        s_scr[h] = states[h]

    @pl.when(t == pl.num_programs(1) - 1)
    def _():
        sfin_ref[...] = s_scr[...]


def _gla_seq(q, k, la, v, gs, s0, gnorm, tl, chunk):
    bsz, seq, _ = q.shape

    def ts(n):
        return pl.BlockSpec((None, tl, n), lambda b, t: (b, t, 0))

    state_shape = (GLA_HEADS, GLA_DK, GLA_DV)
    return pl.pallas_call(
        functools.partial(_gla_seq_body, tl=tl, chunk=chunk),
        out_shape=[jax.ShapeDtypeStruct((bsz, seq, GLA_VAL), F32),
                   jax.ShapeDtypeStruct((bsz,) + state_shape, F32)],
        grid=(bsz, seq // tl),
        in_specs=[ts(GLA_KEY), ts(GLA_KEY), ts(GLA_KEY), ts(GLA_VAL), ts(GLA_VAL),
                  _const_spec(state_shape), _const_spec((1, GLA_VAL))],
        out_specs=[ts(GLA_VAL), pl.BlockSpec((None,) + state_shape, lambda b, t: (b, 0, 0, 0))],
        scratch_shapes=[pltpu.VMEM(state_shape, F32)],
        compiler_params=_cparams(2),
        name="gla_seq",
    )(q, k, la, v, gs, s0, gnorm)


def _gla_step_body(q_ref, k_ref, la_ref, v_ref, gs_ref, s0_ref, gn_ref, *rest, sb, seq):
    o_ref, sfin_ref = rest[-2:]
    for i in range(sb):
        b_all = _chunk_cumsum(la_ref[i], seq)
        for h in range(GLA_HEADS):
            kc = slice(h * GLA_DK, (h + 1) * GLA_DK)
            vc = slice(h * GLA_DV, (h + 1) * GLA_DV)
            o, state = _gla_chunk(q_ref[i, :, kc], k_ref[i, :, kc], v_ref[i, :, vc], b_all[:, kc], s0_ref[i, h])
            o_ref[i, :, vc] = _head_norm_gate(o, gn_ref[:, vc], gs_ref[i, :, vc])
            sfin_ref[i, h] = state


def _layer_slab(sb, tail, layer):
    zeros = (0,) * len(tail)
    return pl.BlockSpec((None, sb) + tail, lambda i: (layer, i) + zeros)


def _alias_prev(in_specs, args, prev_out, out_index):
    if prev_out is None:
        return {}
    in_specs.append(pl.BlockSpec(memory_space=pl.ANY))
    args.append(prev_out)
    return {len(args) - 1: out_index}


def _gla_step(q, k, la, v, gs, s0_all, gnorm, sb, layer, prev_out):
    bsz, seq, _ = q.shape

    def bs(n):
        return pl.BlockSpec((sb, seq, n), lambda i: (i, 0, 0))

    st = _layer_slab(sb, (GLA_HEADS, GLA_DK, GLA_DV), layer)
    in_specs = [bs(GLA_KEY), bs(GLA_KEY), bs(GLA_KEY), bs(GLA_VAL), bs(GLA_VAL), st, _const_spec((1, GLA_VAL))]
    args = [q, k, la, v, gs, s0_all, gnorm]
    aliases = _alias_prev(in_specs, args, prev_out, 1)
    return pl.pallas_call(
        functools.partial(_gla_step_body, sb=sb, seq=seq),
        out_shape=[jax.ShapeDtypeStruct((bsz, seq, GLA_VAL), F32), jax.ShapeDtypeStruct(s0_all.shape, F32)],
        grid=(bsz // sb,),
        in_specs=in_specs,
        out_specs=[bs(GLA_VAL), st],
        input_output_aliases=aliases,
        compiler_params=_cparams(1),
        name="gla_step",
    )(*args)


def _conv_step_body(u_ref, cache_ref, wdw_ref, bdw_ref, cn_ref, *rest, sb, seq):
    c_ref, ncache_ref, ubuf = rest[-3:]
    pad = HIST_ROWS - CONV_HIST
    for i in range(sb):
        ubuf[i, 0:pad, :] = jnp.zeros((pad, D_MODEL), F32)
        ubuf[i, pad:HIST_ROWS, :] = cache_ref[i]
        ubuf[i, HIST_ROWS:HIST_ROWS + seq, :] = u_ref[i]
        acc = jnp.broadcast_to(bdw_ref[...], (seq, D_MODEL))
        for j in range(CONV_WIDTH):
            acc = acc + wdw_ref[j:j + 1, :] * ubuf[i, j + pad:j + pad + seq, :]
        c_ref[i] = _silu(_rmsnorm(acc, cn_ref[...]))
        ncache_ref[i] = ubuf[i, seq + pad:seq + HIST_ROWS, :]


def _conv_step(u, cache_all, wdw, bdw, cnorm, sb, layer, prev_out):
    bsz, seq, _ = u.shape
    us = pl.BlockSpec((sb, seq, D_MODEL), lambda i: (i, 0, 0))
    cs = _layer_slab(sb, (CONV_HIST, D_MODEL), layer)
    in_specs = [us, cs, _const_spec((CONV_WIDTH, D_MODEL)), _const_spec((1, D_MODEL)), _const_spec((1, D_MODEL))]
    args = [u, cache_all, wdw, bdw, cnorm]
    aliases = _alias_prev(in_specs, args, prev_out, 1)
    return pl.pallas_call(
        functools.partial(_conv_step_body, sb=sb, seq=seq),
        out_shape=[jax.ShapeDtypeStruct((bsz, seq, D_MODEL), F32), jax.ShapeDtypeStruct(cache_all.shape, F32)],
        grid=(bsz // sb,),
        in_specs=in_specs,
        out_specs=[us, cs],
        input_output_aliases=aliases,
        scratch_shapes=[pltpu.VMEM((sb, seq + HIST_ROWS, D_MODEL), F32)],
        compiler_params=_cparams(1),
        name="conv_step",
    )(*args)


def _conv_piece(ubuf, wdw_ref, r0, l0):
    pad = HIST_ROWS - CONV_HIST
    xs = ubuf[r0:r0 + CONV_RB + HIST_ROWS, l0:l0 + LANE]
    out = None
    for s in range(8):
        rows = CONV_RB if s == 0 else CONV_RB + 8
        z = None
        for a in range(HIST_ROWS // 8 + 1):
            j = 8 * a + s - pad
            if 0 <= j < CONV_WIDTH:
                term = wdw_ref[j:j + 1, l0:l0 + LANE] * xs[8 * a:8 * a + rows]
                z = term if z is None else z + term
        z = z[s:s + CONV_RB]
        out = z if out is None else out + z
    return out


def _merge_math(h, c_act, og, sa, sb, wpw_ref, wout_ref):
    o_conv = _dot(c_act, wpw_ref[...])
    merged = (sa * og + sb * o_conv).astype(BF16)
    return h + _dot(merged, wout_ref[...])


def _merge_body(h_ref, c_ref, og_ref, sa_ref, sb_ref, wpw_ref, wout_ref, o_ref):
    o_ref[...] = _merge_math(h_ref[...], c_ref[...].astype(BF16), og_ref[...], sa_ref[...], sb_ref[...],
                             wpw_ref, wout_ref)


def _merge(h, c, og, sa, sb, wpw, wout, tm):
    rows = h.shape[0]
    row_spec = pl.BlockSpec((tm, D_MODEL), lambda i: (i, 0))
    return pl.pallas_call(
        _merge_body,
        out_shape=jax.ShapeDtypeStruct((rows, D_MODEL), F32),
        grid=(rows // tm,),
        in_specs=[row_spec] * 5 + [_const_spec((D_MODEL, D_MODEL))] * 2,
        out_specs=row_spec,
        compiler_params=_cparams(1),
        name="merge",
    )(h, c, og, sa, sb, wpw, wout)


def _conv_merge_body(u_ref, cache_ref, h_ref, og_ref, sa_ref, sb_ref, wdw_ref, bdw_ref, cn_ref, wpw_ref, wout_ref,
                     o_ref, ncache_ref, ubuf, cbuf, *, tl):
    t = pl.program_id(1)
    pad = HIST_ROWS - CONV_HIST

    @pl.when(t == 0)
    def _():
        ubuf[0:pad, :] = jnp.zeros((pad, D_MODEL), F32)
        ubuf[pad:HIST_ROWS, :] = cache_ref[0]

    @pl.when(t > 0)
    def _():
        ubuf[0:HIST_ROWS, :] = ubuf[tl:tl + HIST_ROWS, :]

    ubuf[HIST_ROWS:HIST_ROWS + tl, :] = u_ref[...]
    for rb in range(tl // CONV_RB):
        rows = slice(rb * CONV_RB, (rb + 1) * CONV_RB)
        for lc in range(D_MODEL // LANE):
            lanes = slice(lc * LANE, (lc + 1) * LANE)
            cbuf[rows, lanes] = _conv_piece(ubuf, wdw_ref, rb * CONV_RB, lc * LANE) + bdw_ref[:, lanes]
    c_act = _silu(_rmsnorm(cbuf[...], cn_ref[...])).astype(BF16)
    o_ref[...] = _merge_math(h_ref[...], c_act, og_ref[...], sa_ref[...], sb_ref[...], wpw_ref, wout_ref)

    @pl.when(t == pl.num_programs(1) - 1)
    def _():
        ncache_ref[...] = ubuf[tl + pad:tl + HIST_ROWS, :]


def _conv_merge(u, cache, h, og, sa, sb, conv_w, merge_w, tl):
    bsz, seq, _ = h.shape
    tile = pl.BlockSpec((None, tl, D_MODEL), lambda b, t: (b, t, 0))
    vec = _const_spec((1, D_MODEL))
    sq = _const_spec((D_MODEL, D_MODEL))
    return pl.pallas_call(
        functools.partial(_conv_merge_body, tl=tl),
        out_shape=[jax.ShapeDtypeStruct((bsz, seq, D_MODEL), F32),
                   jax.ShapeDtypeStruct((bsz, CONV_HIST, D_MODEL), F32)],
        grid=(bsz, seq // tl),
        in_specs=[tile, _const_spec((1, CONV_HIST, D_MODEL)), tile, tile, tile, tile,
                  _const_spec((CONV_WIDTH, D_MODEL)), vec, vec, sq, sq],
        out_specs=[tile, pl.BlockSpec((None, CONV_HIST, D_MODEL), lambda b, t: (b, 0, 0))],
        scratch_shapes=[pltpu.VMEM((tl + HIST_ROWS, D_MODEL), F32), pltpu.VMEM((tl, D_MODEL), F32)],
        compiler_params=_cparams(2),
        name="conv_merge",
    )(u, cache, h, og, sa, sb, *conv_w, *merge_w)


def _final_norm_body(h_ref, g_ref, o_ref):
    o_ref[...] = _rmsnorm(h_ref[...], g_ref[...])


def _final_norm(h, g, tm):
    rows = h.shape[0]
    row_spec = pl.BlockSpec((tm, D_MODEL), lambda i: (i, 0))
    return pl.pallas_call(
        _final_norm_body,
        out_shape=jax.ShapeDtypeStruct((rows, D_MODEL), F32),
        grid=(rows // tm,),
        in_specs=[row_spec, _const_spec((1, D_MODEL))],
        out_specs=row_spec,
        compiler_params=_cparams(1),
        name="final_norm",
    )(h, g)


def kernel(x_prompt, x_sample, state_gla, cache_conv, meta_tokens, norm_ffn1, w_ffn1_gate, w_ffn1_up, w_ffn1_down, norm_mix, w_in, w_decay_up, b_decay, gla_norm, w_dw, b_dw, conv_norm, w_pw, w_out, norm_ffn2, w_ffn2_gate, w_ffn2_up, w_ffn2_down, norm_final):
    n_prompt, l_prompt, _ = x_prompt.shape
    n_sample, l_sample, _ = x_sample.shape
    rows_p = n_prompt * l_prompt
    rows_s = n_sample * l_sample
    rows_small = rows_s + N_META

    hp = x_prompt.reshape(rows_p, D_MODEL)
    hs = jnp.concatenate([x_sample.reshape(rows_s, D_MODEL), meta_tokens], axis=0)

    tm_ffn = (512, rows_small)
    tm_proj = (256, 208)
    tm_merge_small = rows_small
    tl_prompt = 512
    tl_conv = 512

    lr0 = 2 * GLA_KEY + 2 * GLA_VAL
    w_main = jnp.concatenate([w_in[:, :, :lr0], w_in[:, :, lr0 + GATE_RANK:]], axis=2).astype(BF16)
    w_lr = jnp.pad(w_in[:, :, lr0:lr0 + GATE_RANK], ((0, 0), (0, 0), (0, LANE - GATE_RANK))).astype(BF16)
    w_du = jnp.pad(w_decay_up, ((0, 0), (0, LANE - GATE_RANK), (0, 0))).astype(BF16)

    zero_state = jnp.zeros((1, 1, GLA_HEADS, GLA_DK, GLA_DV), F32)
    zero_cache = jnp.zeros((1, 1, CONV_HIST, D_MODEL), F32)

    sp_list, cp_list = [], []
    state_s = cache_s = None
    for l in range(DEPTH):
        def row(a):
            return a[l].reshape(1, -1)

        ffn1_w = (row(norm_ffn1), w_ffn1_gate[l].astype(BF16), w_ffn1_up[l].astype(BF16), w_ffn1_down[l].astype(BF16))
        ffn2_w = (row(norm_ffn2), w_ffn2_gate[l].astype(BF16), w_ffn2_up[l].astype(BF16), w_ffn2_down[l].astype(BF16))
        proj_w = (row(norm_mix), w_main[l], w_lr[l], w_du[l], row(b_decay))
        conv_w = (w_dw[l], row(b_dw), row(conv_norm))
        merge_w = (w_pw[l].astype(BF16), w_out[l].astype(BF16))
        gnorm = row(gla_norm)

        hp = _ffn(hp, *ffn1_w, tm_ffn[0])
        hs = _ffn(hs, *ffn1_w, tm_ffn[1])

        q, k, la, v, gs, u, sa, sb = _proj(hs, *proj_w, tm_proj[1])

        def seqs(a, lo, n, length):
            return a[lo:lo + n * length].reshape(n, length, a.shape[-1])

        samp = [seqs(a, 0, n_sample, l_sample) for a in (q, k, la, v, gs)]
        meta = [seqs(a, rows_s, 1, N_META) for a in (q, k, la, v, gs)]
        og_s, state_s = _gla_step(*samp, state_gla, gnorm, 8, l, state_s)
        og_m, s_m = _gla_step(*meta, zero_state, gnorm, 1, 0, None)
        c_s, cache_s = _conv_step(seqs(u, 0, n_sample, l_sample), cache_conv, *conv_w, 16, l, cache_s)
        c_m, nc_m = _conv_step(seqs(u, rows_s, 1, N_META), zero_cache, *conv_w, 1, 0, None)
        og = jnp.concatenate([og_s.reshape(rows_s, GLA_VAL), og_m.reshape(N_META, GLA_VAL)], axis=0)
        cc = jnp.concatenate([c_s.reshape(rows_s, D_MODEL), c_m.reshape(N_META, D_MODEL)], axis=0)
        hs = _merge(hs, cc, og, sa, sb, *merge_w, tm_merge_small)
        hs = _ffn(hs, *ffn2_w, tm_ffn[1])

        q, k, la, v, gs, u, sa, sb = _proj(hp, *proj_w, tm_proj[0])

        def pseq(a):
            return a.reshape(n_prompt, l_prompt, a.shape[-1])

        og_p, s_p = _gla_seq(pseq(q), pseq(k), pseq(la), pseq(v), pseq(gs), s_m[0, 0], gnorm, tl_prompt, CHUNK)
        hp, nc_p = _conv_merge(pseq(u), nc_m[0], pseq(hp), og_p, pseq(sa), pseq(sb), conv_w, merge_w, tl_conv)
        hp = _ffn(hp.reshape(rows_p, D_MODEL), *ffn2_w, tm_ffn[0])
        sp_list.append(s_p)
        cp_list.append(nc_p)

    gfin = norm_final.reshape(1, D_MODEL)
    y_prompt = _final_norm(hp, gfin, 512).reshape(n_prompt, l_prompt, D_MODEL)
    y_sample = _final_norm(hs, gfin, rows_small)[:rows_s].reshape(n_sample, l_sample, D_MODEL)
    return (y_prompt, y_sample, jnp.stack(sp_list), jnp.stack(cp_list), state_s, cache_s)
```

```python
import functools

import jax
import jax.numpy as jnp
from jax import lax
from jax.experimental import pallas as pl
from jax.experimental.pallas import tpu as pltpu

F32 = jnp.float32
BF16 = jnp.bfloat16

D_MODEL = 1024
DEPTH = 4
N_META = 16
GLA_HEADS = 4
GLA_DK = 128
GLA_DV = 256
GLA_KEY = GLA_HEADS * GLA_DK
GLA_VAL = GLA_HEADS * GLA_DV
GATE_RANK = 16
GATE_TAU = 16.0
CHUNK = 64
CONV_WIDTH = 31
CONV_HIST = CONV_WIDTH - 1
D_FF = 2816
EPS = 1e-6

LANE = 128
FF_COLS = 256
HIST_ROWS = 32
CONV_RB = 64
VMEM_LIMIT = 56 * 1024 * 1024

STATE_TAIL = (GLA_HEADS, GLA_DK, GLA_DV)
NT_DIMS = (((1,), (1,)), ((), ()))
TN_DIMS = (((0,), (0,)), ((), ()))


def _cparams(n_axes):
    return pltpu.CompilerParams(dimension_semantics=("arbitrary",) * n_axes, vmem_limit_bytes=VMEM_LIMIT)


def _const_spec(shape):
    nd = len(shape)
    return pl.BlockSpec(shape, lambda *_: (0,) * nd, pipeline_mode=pl.Buffered(1))


def _layer_spec(shape, layer):
    nd = len(shape)
    return pl.BlockSpec((None,) + tuple(shape), lambda *_: (layer,) + (0,) * nd, pipeline_mode=pl.Buffered(1))


def _alias_prev(in_specs, args, prev_out, out_index):
    if prev_out is None:
        return {}
    in_specs.append(pl.BlockSpec(memory_space=pl.ANY))
    args.append(prev_out)
    return {len(args) - 1: out_index}


def _rmsnorm(x, g):
    return x * lax.rsqrt(jnp.mean(x * x, axis=-1, keepdims=True) + EPS) * g


def _sigmoid(x):
    return 1.0 / (1.0 + jnp.exp(-x))


def _silu(x):
    return x * _sigmoid(x)


def _log_sigmoid(x):
    return jnp.minimum(x, 0.0) - jnp.log1p(jnp.exp(-jnp.abs(x)))


def _dot(a, b):
    return jnp.dot(a, b, preferred_element_type=F32)


def _ffn_math(x, g, wg_ref, wu_ref, wd_ref):
    xn = _rmsnorm(x, g).astype(BF16)
    acc = None
    for c in range(D_FF // FF_COLS):
        cols = slice(c * FF_COLS, (c + 1) * FF_COLS)
        gate = _dot(xn, wg_ref[:, cols])
        up = _dot(xn, wu_ref[:, cols])
        act = (_silu(gate) * up).astype(BF16)
        part = _dot(act, wd_ref[cols, :])
        acc = part if acc is None else acc + part
    return x + 0.5 * acc


def _ffn_body(h_ref, g_ref, wg_ref, wu_ref, wd_ref, *rest):
    o_ref = rest[-1]
    y = _ffn_math(h_ref[...], g_ref[...], wg_ref, wu_ref, wd_ref)
    if len(rest) == 2:
        y = _rmsnorm(y, rest[0][...])
    o_ref[...] = y


def _ffn(h, w, layer, tm, final_gain=None):
    rows = h.shape[0]
    row_spec = pl.BlockSpec((tm, D_MODEL), lambda i: (i, 0))
    in_specs = [row_spec, _layer_spec((1, D_MODEL), layer), _layer_spec((D_MODEL, D_FF), layer),
                _layer_spec((D_MODEL, D_FF), layer), _layer_spec((D_FF, D_MODEL), layer)]
    args = [h, *w]
    if final_gain is not None:
        in_specs.append(_const_spec((1, D_MODEL)))
        args.append(final_gain)
    return pl.pallas_call(
        _ffn_body,
        out_shape=jax.ShapeDtypeStruct((rows, D_MODEL), F32),
        grid=(rows // tm,),
        in_specs=in_specs,
        out_specs=row_spec,
        compiler_params=_cparams(1),
        name="ffn",
    )(*args)


_OFF_Q, _OFF_K, _OFF_V, _OFF_G, _OFF_A, _OFF_GA, _OFF_MA, _OFF_MB, _OFF_END = (
    0, 512, 1024, 2048, 3072, 4096, 5120, 6144, 7168)


def _proj_body(h_ref, g_ref, wm_ref, wlr_ref, wdu_ref, bdec_ref,
               q_ref, k_ref, la_ref, v_ref, gs_ref, u_ref, sa_ref, sb_ref):
    xn = _rmsnorm(h_ref[...], g_ref[...]).astype(BF16)

    def mm(lo, hi):
        return _dot(xn, wm_ref[:, lo:hi])

    q_ref[...] = mm(_OFF_Q, _OFF_K) * (GLA_DK ** -0.5)
    k_ref[...] = mm(_OFF_K, _OFF_V)
    v_ref[...] = mm(_OFF_V, _OFF_G).astype(BF16)
    gs_ref[...] = _silu(mm(_OFF_G, _OFF_A)).astype(BF16)
    u_ref[...] = mm(_OFF_A, _OFF_GA) * _sigmoid(mm(_OFF_GA, _OFF_MA))
    sa_ref[...] = _sigmoid(mm(_OFF_MA, _OFF_MB)).astype(BF16)
    sb_ref[...] = _sigmoid(mm(_OFF_MB, _OFF_END)).astype(BF16)
    a_lr = _dot(xn, wlr_ref[...]).astype(BF16)
    z = _dot(a_lr, wdu_ref[...]) + bdec_ref[...]
    la_ref[...] = _log_sigmoid(z) / GATE_TAU


def _proj(h, w, layer, tm):
    rows = h.shape[0]

    def rs(n):
        return pl.BlockSpec((tm, n), lambda i: (i, 0))

    outs = ((GLA_KEY, F32), (GLA_KEY, F32), (GLA_KEY, F32), (GLA_VAL, BF16), (GLA_VAL, BF16),
            (D_MODEL, F32), (D_MODEL, BF16), (D_MODEL, BF16))
    return pl.pallas_call(
        _proj_body,
        out_shape=[jax.ShapeDtypeStruct((rows, n), dt) for n, dt in outs],
        grid=(rows // tm,),
        in_specs=[rs(D_MODEL), _layer_spec((1, D_MODEL), layer), _layer_spec((D_MODEL, _OFF_END), layer),
                  _layer_spec((D_MODEL, LANE), layer), _layer_spec((LANE, GLA_KEY), layer),
                  _layer_spec((1, GLA_KEY), layer)],
        out_specs=[rs(n) for n, _ in outs],
        compiler_params=_cparams(1),
        name="proj",
    )(h, *w)


def _chunk_cumsum(x, chunk):
    pos = lax.broadcasted_iota(jnp.int32, x.shape, 0) % chunk
    shift = 1
    while shift < chunk:
        x = x + jnp.where(pos >= shift, pltpu.roll(x, shift, 0), 0.0)
        shift *= 2
    return x


def _head_norm_gate(o, gnorm, gs):
    return o * lax.rsqrt(jnp.mean(o * o, axis=-1, keepdims=True) + EPS) * gnorm * gs


def _gla_blocks(q, k, v, la, gs, gnorm, states, blk, store):
    n_blk = q.shape[0] // blk
    heads = range(GLA_HEADS)
    kcs = [slice(h * GLA_DK, (h + 1) * GLA_DK) for h in heads]
    vcs = [slice(h * GLA_DV, (h + 1) * GLA_DV) for h in heads]
    carried = not isinstance(states[0], list)
    b_all = _chunk_cumsum(la, blk)
    q_t, k_t, k_u, v_b, dec = [], [], [], [], []
    for j in range(n_blk):
        rows = slice(j * blk, (j + 1) * blk)
        b = b_all[rows, :]
        b_last = b[blk - 1:blk, :]
        kj = k[rows, :]
        q_t.append((q[rows, :] * jnp.exp(b)).astype(BF16))
        k_t.append((kj * jnp.exp(-b)).astype(BF16))
        k_u.append((kj * jnp.exp(b_last - b)).astype(BF16))
        v_b.append(v[rows, :])
        dec.append(jnp.exp(b_last))
    tri = lax.broadcasted_iota(jnp.int32, (blk, blk), 0) >= lax.broadcasted_iota(jnp.int32, (blk, blk), 1)
    eye = (lax.broadcasted_iota(jnp.int32, (GLA_DK, GLA_DK), 0)
           == lax.broadcasted_iota(jnp.int32, (GLA_DK, GLA_DK), 1))
    probs = [[jnp.where(tri, lax.dot_general(q_t[j][:, kcs[h]], k_t[j][:, kcs[h]], NT_DIMS,
                                            preferred_element_type=F32), 0.0).astype(BF16)
              for h in heads] for j in range(n_blk)]
    upd = [[lax.dot_general(k_u[j][:, kcs[h]], v_b[j][:, vcs[h]], TN_DIMS, preferred_element_type=F32)
            for h in heads] for j in range(n_blk)]
    dec_col = [[jnp.sum(jnp.where(eye, jnp.broadcast_to(dec[j][:, kcs[h]], (GLA_DK, GLA_DK)), 0.0),
                        axis=1, keepdims=True) for h in heads] for j in range(n_blk)]
    finals = []
    cur = list(states) if carried else None
    for j in range(n_blk):
        rows = slice(j * blk, (j + 1) * blk)
        if not carried:
            cur = list(states[j])
        for h in heads:
            o = _dot(probs[j][h], v_b[j][:, vcs[h]]) + _dot(q_t[j][:, kcs[h]], cur[h].astype(BF16))
            cur[h] = dec_col[j][h] * cur[h] + upd[j][h]
            store(j, h, _head_norm_gate(o, gnorm[:, vcs[h]], gs[rows, vcs[h]]))
        finals.append(list(cur))
    return finals


def _gla_seq_body(q_ref, k_ref, la_ref, v_ref, gs_ref, s0_ref, gn_ref, o_ref, sfin_ref, s_scr, *, chunk):
    t = pl.program_id(1)

    @pl.when(t == 0)
    def _():
        s_scr[...] = s0_ref[...]

    def store(j, h, o):
        o_ref[j * chunk:(j + 1) * chunk, h * GLA_DV:(h + 1) * GLA_DV] = o.astype(o_ref.dtype)

    finals = _gla_blocks(q_ref[...], k_ref[...], v_ref[...], la_ref[...], gs_ref[...].astype(F32), gn_ref[...],
                         [s_scr[h] for h in range(GLA_HEADS)], chunk, store)
    for h in range(GLA_HEADS):
        s_scr[h] = finals[-1][h]

    @pl.when(t == pl.num_programs(1) - 1)
    def _():
        sfin_ref[...] = s_scr[...]


def _gla_seq(q, k, la, v, gs, s0, gnorm, layer, tl, chunk):
    bsz, seq, _ = q.shape

    def ts(n):
        return pl.BlockSpec((None, tl, n), lambda b, t: (b, t, 0))

    return pl.pallas_call(
        functools.partial(_gla_seq_body, chunk=chunk),
        out_shape=[jax.ShapeDtypeStruct((bsz, seq, GLA_VAL), BF16),
                   jax.ShapeDtypeStruct((bsz,) + STATE_TAIL, F32)],
        grid=(bsz, seq // tl),
        in_specs=[ts(GLA_KEY), ts(GLA_KEY), ts(GLA_KEY), ts(GLA_VAL), ts(GLA_VAL),
                  _const_spec(STATE_TAIL), _layer_spec((1, GLA_VAL), layer)],
        out_specs=[ts(GLA_VAL), pl.BlockSpec((None,) + STATE_TAIL, lambda b, t: (b, 0, 0, 0))],
        scratch_shapes=[pltpu.VMEM(STATE_TAIL, F32)],
        compiler_params=_cparams(2),
        name="gla_seq",
    )(q, k, la, v, gs, s0, gnorm)


def _gla_step_body(q_ref, k_ref, la_ref, v_ref, gs_ref, s0_ref, gn_ref, *rest, sb, seq):
    o_ref, sfin_ref = rest[-2:]

    def store(j, h, o):
        o_ref[j * seq:(j + 1) * seq, h * GLA_DV:(h + 1) * GLA_DV] = o.astype(o_ref.dtype)

    states = [[s0_ref[i, h] for h in range(GLA_HEADS)] for i in range(sb)]
    finals = _gla_blocks(q_ref[...], k_ref[...], v_ref[...], la_ref[...], gs_ref[...].astype(F32), gn_ref[...],
                         states, seq, store)
    for i in range(sb):
        for h in range(GLA_HEADS):
            sfin_ref[i, h] = finals[i][h]


def _gla_step(q, k, la, v, gs, s0_all, gnorm, layer, state_layer, seq, sb, row0, n_seq, prev_og, prev_state):
    rows = q.shape[0]
    blk = sb * seq
    base = row0 // blk

    def bs(n):
        return pl.BlockSpec((blk, n), lambda i: (base + i, 0))

    st = pl.BlockSpec((None, sb) + STATE_TAIL, lambda i: (state_layer, i, 0, 0, 0))
    in_specs = [bs(GLA_KEY), bs(GLA_KEY), bs(GLA_KEY), bs(GLA_VAL), bs(GLA_VAL), st, _layer_spec((1, GLA_VAL), layer)]
    args = [q, k, la, v, gs, s0_all, gnorm]
    aliases = _alias_prev(in_specs, args, prev_og, 0)
    aliases.update(_alias_prev(in_specs, args, prev_state, 1))
    return pl.pallas_call(
        functools.partial(_gla_step_body, sb=sb, seq=seq),
        out_shape=[jax.ShapeDtypeStruct((rows, GLA_VAL), F32), jax.ShapeDtypeStruct(s0_all.shape, F32)],
        grid=(n_seq // sb,),
        in_specs=in_specs,
        out_specs=[bs(GLA_VAL), st],
        input_output_aliases=aliases,
        compiler_params=_cparams(1),
        name="gla_step",
    )(*args)


def _conv_act(acc, cn):
    return _silu(_rmsnorm(acc, cn))


def _conv_step_body(u_ref, cache_ref, wdw_ref, bdw_ref, cn_ref, *rest, sb, seq):
    c_ref, ncache_ref = rest[-2:]
    u_t = jnp.swapaxes(u_ref[...].reshape(sb, seq, D_MODEL), 0, 1)

    def up(m):
        return cache_ref[m] if m < CONV_HIST else u_t[m - CONV_HIST]

    outs = []
    for t in range(seq):
        acc = jnp.broadcast_to(bdw_ref[...], (sb, D_MODEL))
        for j in range(CONV_WIDTH):
            acc = acc + wdw_ref[j:j + 1, :] * up(t + j)
        outs.append(_conv_act(acc, cn_ref[...]))
    c_ref[...] = jnp.swapaxes(jnp.stack(outs), 0, 1).reshape(sb * seq, D_MODEL)
    for m in range(CONV_HIST):
        ncache_ref[m] = up(m + seq)


def _conv_step(u, cache_all, conv_w, layer, cache_layer, seq, sb, row0, n_seq, prev_c, prev_cache):
    rows = u.shape[0]
    blk = sb * seq
    base = row0 // blk
    us = pl.BlockSpec((blk, D_MODEL), lambda i: (base + i, 0))
    cs = pl.BlockSpec((None, CONV_HIST, sb, D_MODEL), lambda i: (cache_layer, 0, i, 0))
    in_specs = [us, cs, _layer_spec((CONV_WIDTH, D_MODEL), layer), _layer_spec((1, D_MODEL), layer),
                _layer_spec((1, D_MODEL), layer)]
    args = [u, cache_all, *conv_w]
    aliases = _alias_prev(in_specs, args, prev_c, 0)
    aliases.update(_alias_prev(in_specs, args, prev_cache, 1))
    return pl.pallas_call(
        functools.partial(_conv_step_body, sb=sb, seq=seq),
        out_shape=[jax.ShapeDtypeStruct((rows, D_MODEL), F32), jax.ShapeDtypeStruct(cache_all.shape, F32)],
        grid=(n_seq // sb,),
        in_specs=in_specs,
        out_specs=[us, cs],
        input_output_aliases=aliases,
        compiler_params=_cparams(1),
        name="conv_step",
    )(*args)


def _conv_piece(ubuf, wdw_ref, r0, l0):
    pad = HIST_ROWS - CONV_HIST
    xs = ubuf[r0:r0 + CONV_RB + HIST_ROWS, l0:l0 + LANE]
    out = None
    for s in range(8):
        rows = CONV_RB if s == 0 else CONV_RB + 8
        z = None
        for a in range(HIST_ROWS // 8 + 1):
            j = 8 * a + s - pad
            if 0 <= j < CONV_WIDTH:
                term = wdw_ref[j:j + 1, l0:l0 + LANE] * xs[8 * a:8 * a + rows]
                z = term if z is None else z + term
        z = z[s:s + CONV_RB]
        out = z if out is None else out + z
    return out


def _merge_math(h, c_act, og, sa, sb, wpw_ref, wout_ref):
    o_conv = _dot(c_act, wpw_ref[...])
    merged = (sa * og + sb * o_conv).astype(BF16)
    return h + _dot(merged, wout_ref[...])


def _merge_body(h_ref, c_ref, og_ref, sa_ref, sb_ref, wpw_ref, wout_ref, o_ref):
    o_ref[...] = _merge_math(h_ref[...], c_ref[...].astype(BF16), og_ref[...], sa_ref[...].astype(F32),
                             sb_ref[...].astype(F32), wpw_ref, wout_ref)


def _merge(h, c, og, sa, sb, merge_w, layer, tm):
    rows = h.shape[0]
    row_spec = pl.BlockSpec((tm, D_MODEL), lambda i: (i, 0))
    sq = _layer_spec((D_MODEL, D_MODEL), layer)
    return pl.pallas_call(
        _merge_body,
        out_shape=jax.ShapeDtypeStruct((rows, D_MODEL), F32),
        grid=(rows // tm,),
        in_specs=[row_spec] * 5 + [sq, sq],
        out_specs=row_spec,
        compiler_params=_cparams(1),
        name="merge",
    )(h, c, og, sa, sb, *merge_w)


def _conv_merge_body(u_ref, cache_ref, h_ref, og_ref, sa_ref, sb_ref, wdw_ref, bdw_ref, cn_ref, wpw_ref, wout_ref,
                     o_ref, ncache_ref, ubuf, cbuf, *, tl):
    t = pl.program_id(1)
    pad = HIST_ROWS - CONV_HIST

    @pl.when(t == 0)
    def _():
        ubuf[0:pad, :] = jnp.zeros((pad, D_MODEL), F32)
        ubuf[pad:HIST_ROWS, :] = cache_ref[...]

    @pl.when(t > 0)
    def _():
        ubuf[0:HIST_ROWS, :] = ubuf[tl:tl + HIST_ROWS, :]

    ubuf[HIST_ROWS:HIST_ROWS + tl, :] = u_ref[...]
    for rb in range(tl // CONV_RB):
        rows = slice(rb * CONV_RB, (rb + 1) * CONV_RB)
        for lc in range(D_MODEL // LANE):
            lanes = slice(lc * LANE, (lc + 1) * LANE)
            cbuf[rows, lanes] = _conv_piece(ubuf, wdw_ref, rb * CONV_RB, lc * LANE) + bdw_ref[:, lanes]
    c_act = _conv_act(cbuf[...], cn_ref[...]).astype(BF16)
    o_ref[...] = _merge_math(h_ref[...], c_act, og_ref[...].astype(F32), sa_ref[...].astype(F32),
                             sb_ref[...].astype(F32), wpw_ref, wout_ref)

    @pl.when(t == pl.num_programs(1) - 1)
    def _():
        ncache_ref[...] = ubuf[tl + pad:tl + HIST_ROWS, :]


def _conv_merge(u, cache, h, og, sa, sb, conv_w, merge_w, layer, tl):
    bsz, seq, _ = h.shape
    tile = pl.BlockSpec((None, tl, D_MODEL), lambda b, t: (b, t, 0))
    vec = _layer_spec((1, D_MODEL), layer)
    sq = _layer_spec((D_MODEL, D_MODEL), layer)
    return pl.pallas_call(
        functools.partial(_conv_merge_body, tl=tl),
        out_shape=[jax.ShapeDtypeStruct((bsz, seq, D_MODEL), F32),
                   jax.ShapeDtypeStruct((bsz, CONV_HIST, D_MODEL), F32)],
        grid=(bsz, seq // tl),
        in_specs=[tile, _const_spec((CONV_HIST, D_MODEL)), tile, tile, tile, tile,
                  _layer_spec((CONV_WIDTH, D_MODEL), layer), vec, vec, sq, sq],
        out_specs=[tile, pl.BlockSpec((None, CONV_HIST, D_MODEL), lambda b, t: (b, 0, 0))],
        scratch_shapes=[pltpu.VMEM((tl + HIST_ROWS, D_MODEL), F32), pltpu.VMEM((tl, D_MODEL), F32)],
        compiler_params=_cparams(2),
        name="conv_merge",
    )(u, cache, h, og, sa, sb, *conv_w, *merge_w)


def kernel(x_prompt, x_sample, state_gla, cache_conv, meta_tokens, norm_ffn1, w_ffn1_gate, w_ffn1_up, w_ffn1_down, norm_mix, w_in, w_decay_up, b_decay, gla_norm, w_dw, b_dw, conv_norm, w_pw, w_out, norm_ffn2, w_ffn2_gate, w_ffn2_up, w_ffn2_down, norm_final):
    n_prompt, l_prompt, _ = x_prompt.shape
    n_sample, l_sample, _ = x_sample.shape
    rows_p = n_prompt * l_prompt
    rows_s = n_sample * l_sample
    rows_small = rows_s + N_META

    hp = x_prompt.reshape(rows_p, D_MODEL)
    hs = jnp.concatenate([x_sample.reshape(rows_s, D_MODEL), meta_tokens], axis=0)

    tm_ffn_p, tm_proj_p, tl_prompt = 1024, 512, 512
    tm_proj_s = 208
    sb_gla, sb_conv = 8, 32

    def vec(a):
        return a.reshape(a.shape[0], 1, a.shape[-1])

    def bf(a):
        return a.astype(BF16)

    lr0 = 2 * GLA_KEY + 2 * GLA_VAL
    w_main = jnp.concatenate([w_in[:, :, :lr0], w_in[:, :, lr0 + GATE_RANK:]], axis=2).astype(BF16)
    w_lr = jnp.pad(w_in[:, :, lr0:lr0 + GATE_RANK], ((0, 0), (0, 0), (0, LANE - GATE_RANK))).astype(BF16)
    w_du = jnp.pad(w_decay_up, ((0, 0), (0, LANE - GATE_RANK), (0, 0))).astype(BF16)

    ffn1_w = (vec(norm_ffn1), bf(w_ffn1_gate), bf(w_ffn1_up), bf(w_ffn1_down))
    ffn2_w = (vec(norm_ffn2), bf(w_ffn2_gate), bf(w_ffn2_up), bf(w_ffn2_down))
    proj_w = (vec(norm_mix), w_main, w_lr, w_du, vec(b_decay))
    conv_w = (w_dw, vec(b_dw), vec(conv_norm))
    merge_w = (bf(w_pw), bf(w_out))
    gnorm = vec(gla_norm)
    gfin = norm_final.reshape(1, D_MODEL)

    zero_state = jnp.zeros((1, 1) + STATE_TAIL, F32)
    zero_cache = jnp.zeros((1, CONV_HIST, 1, D_MODEL), F32)
    cache_tm = jnp.swapaxes(cache_conv, 1, 2)

    def pseq(a):
        return a.reshape(n_prompt, l_prompt, a.shape[-1])

    sp_list, cp_list = [], []
    state_s = cache_s = None
    for l in range(DEPTH):
        last = l == DEPTH - 1
        hp = _ffn(hp, ffn1_w, l, tm_ffn_p)
        hs = _ffn(hs, ffn1_w, l, rows_small)

        q, k, la, v, gs, u, sa, sb = _proj(hs, proj_w, l, tm_proj_s)
        og, state_s = _gla_step(q, k, la, v, gs, state_gla, gnorm, l, l, l_sample, sb_gla, 0, n_sample,
                                None, state_s)
        og, s_m = _gla_step(q, k, la, v, gs, zero_state, gnorm, l, 0, N_META, 1, rows_s, 1, og, None)
        cc, cache_s = _conv_step(u, cache_tm, conv_w, l, l, l_sample, sb_conv, 0, n_sample, None, cache_s)
        cc, nc_m = _conv_step(u, zero_cache, conv_w, l, 0, N_META, 1, rows_s, 1, cc, None)
        hs = _merge(hs, cc, og, sa, sb, merge_w, l, rows_small)
        hs = _ffn(hs, ffn2_w, l, rows_small, gfin if last else None)

        q, k, la, v, gs, u, sa, sb = _proj(hp, proj_w, l, tm_proj_p)
        og_p, s_p = _gla_seq(pseq(q), pseq(k), pseq(la), pseq(v), pseq(gs), s_m[0, 0], gnorm, l, tl_prompt, CHUNK)
        hp, nc_p = _conv_merge(pseq(u), nc_m[0, :, 0, :], pseq(hp), og_p, pseq(sa), pseq(sb), conv_w, merge_w, l,
                               tl_prompt)
        hp = _ffn(hp.reshape(rows_p, D_MODEL), ffn2_w, l, tm_ffn_p, gfin if last else None)
        sp_list.append(s_p)
        cp_list.append(nc_p)

    y_prompt = hp.reshape(n_prompt, l_prompt, D_MODEL)
    y_sample = hs[:rows_s].reshape(n_sample, l_sample, D_MODEL)
    return (y_prompt, y_sample, jnp.stack(sp_list), jnp.stack(cp_list), state_s, jnp.swapaxes(cache_s, 1, 2))
```

```python
import functools

import jax
import jax.numpy as jnp
from jax import lax
from jax.experimental import pallas as pl
from jax.experimental.pallas import tpu as pltpu

F32 = jnp.float32
BF16 = jnp.bfloat16

D_MODEL = 1024
DEPTH = 4
N_META = 16
GLA_HEADS = 4
GLA_DK = 128
GLA_DV = 256
GLA_KEY = GLA_HEADS * GLA_DK
GLA_VAL = GLA_HEADS * GLA_DV
GATE_RANK = 16
GATE_TAU = 16.0
CHUNK = 64
CONV_WIDTH = 31
CONV_HIST = CONV_WIDTH - 1
D_FF = 2816
EPS = 1e-6

LANE = 128
FF_COLS = 256
HIST_ROWS = 32
CONV_RB = 64
VMEM_LIMIT = 56 * 1024 * 1024

STATE_TAIL = (GLA_HEADS, GLA_DK, GLA_DV)
NT_DIMS = (((1,), (1,)), ((), ()))
TN_DIMS = (((0,), (0,)), ((), ()))


def _cparams(n_axes):
    return pltpu.CompilerParams(dimension_semantics=("arbitrary",) * n_axes, vmem_limit_bytes=VMEM_LIMIT)


def _const_spec(shape):
    nd = len(shape)
    return pl.BlockSpec(shape, lambda *_: (0,) * nd, pipeline_mode=pl.Buffered(1))


def _layer_spec(shape, layer):
    nd = len(shape)
    return pl.BlockSpec((None,) + tuple(shape), lambda *_: (layer,) + (0,) * nd, pipeline_mode=pl.Buffered(1))


def _alias_prev(in_specs, args, prev_out, out_index):
    if prev_out is None:
        return {}
    in_specs.append(pl.BlockSpec(memory_space=pl.ANY))
    args.append(prev_out)
    return {len(args) - 1: out_index}


def _rmsnorm(x, g):
    return x * lax.rsqrt(jnp.mean(x * x, axis=-1, keepdims=True) + EPS) * g


def _sigmoid(x):
    return 1.0 / (1.0 + jnp.exp(-x))


def _silu(x):
    return x * _sigmoid(x)


def _log_sigmoid(x):
    return jnp.minimum(x, 0.0) - jnp.log1p(jnp.exp(-jnp.abs(x)))


def _dot(a, b):
    return jnp.dot(a, b, preferred_element_type=F32)


def _ffn_math(x, g, wg_ref, wu_ref, wd_ref):
    xn = _rmsnorm(x, g).astype(BF16)
    acc = None
    for c in range(D_FF // FF_COLS):
        cols = slice(c * FF_COLS, (c + 1) * FF_COLS)
        gate = _dot(xn, wg_ref[:, cols])
        up = _dot(xn, wu_ref[:, cols])
        act = (_silu(gate) * up).astype(BF16)
        part = _dot(act, wd_ref[cols, :])
        acc = part if acc is None else acc + part
    return x + 0.5 * acc


def _ffn_body(h_ref, g_ref, wg_ref, wu_ref, wd_ref, *rest):
    o_ref = rest[-1]
    y = _ffn_math(h_ref[...], g_ref[...], wg_ref, wu_ref, wd_ref)
    if len(rest) == 2:
        y = _rmsnorm(y, rest[0][...])
    o_ref[...] = y


def _ffn(h, w, layer, tm, final_gain=None):
    rows = h.shape[0]
    row_spec = pl.BlockSpec((tm, D_MODEL), lambda i: (i, 0))
    in_specs = [row_spec, _layer_spec((1, D_MODEL), layer), _layer_spec((D_MODEL, D_FF), layer),
                _layer_spec((D_MODEL, D_FF), layer), _layer_spec((D_FF, D_MODEL), layer)]
    args = [h, *w]
    if final_gain is not None:
        in_specs.append(_const_spec((1, D_MODEL)))
        args.append(final_gain)
    return pl.pallas_call(
        _ffn_body,
        out_shape=jax.ShapeDtypeStruct((rows, D_MODEL), F32),
        grid=(rows // tm,),
        in_specs=in_specs,
        out_specs=row_spec,
        compiler_params=_cparams(1),
        name="ffn",
    )(*args)


def _ffn_cols_body(h_ref, g_ref, wg_ref, wu_ref, wd_ref, *rest, final):
    o_ref, xn_scr, acc_scr = rest[-3:]
    c = pl.program_id(0)

    @pl.when(c == 0)
    def _():
        xn_scr[...] = _rmsnorm(h_ref[...], g_ref[...]).astype(BF16)
        acc_scr[...] = jnp.zeros_like(acc_scr)

    xn = xn_scr[...]
    act = (_silu(_dot(xn, wg_ref[...])) * _dot(xn, wu_ref[...])).astype(BF16)
    acc_scr[...] += _dot(act, wd_ref[...])

    @pl.when(c == pl.num_programs(0) - 1)
    def _():
        y = h_ref[...] + 0.5 * acc_scr[...]
        if final:
            y = _rmsnorm(y, rest[0][...])
        o_ref[...] = y


def _ffn_cols(h, w, layer, final_gain=None):
    rows = h.shape[0]
    whole = _const_spec((rows, D_MODEL))
    in_specs = [whole, _layer_spec((1, D_MODEL), layer),
                pl.BlockSpec((None, D_MODEL, FF_COLS), lambda c: (layer, 0, c)),
                pl.BlockSpec((None, D_MODEL, FF_COLS), lambda c: (layer, 0, c)),
                pl.BlockSpec((None, FF_COLS, D_MODEL), lambda c: (layer, c, 0))]
    args = [h, *w]
    if final_gain is not None:
        in_specs.append(_const_spec((1, D_MODEL)))
        args.append(final_gain)
    return pl.pallas_call(
        functools.partial(_ffn_cols_body, final=final_gain is not None),
        out_shape=jax.ShapeDtypeStruct((rows, D_MODEL), F32),
        grid=(D_FF // FF_COLS,),
        in_specs=in_specs,
        out_specs=pl.BlockSpec((rows, D_MODEL), lambda c: (0, 0)),
        scratch_shapes=[pltpu.VMEM((rows, D_MODEL), BF16), pltpu.VMEM((rows, D_MODEL), F32)],
        compiler_params=_cparams(1),
        name="ffn_cols",
    )(*args)


_LR0 = 2 * GLA_KEY + 2 * GLA_VAL
D_IN = _LR0 + GATE_RANK + 2 * D_MODEL + 2 * D_MODEL


def _wprep_body(w_ref, main_ref, lr_ref):
    main_ref[:, 0:_LR0] = w_ref[:, 0:_LR0].astype(BF16)
    main_ref[:, _LR0:] = w_ref[:, _LR0 + GATE_RANK:].astype(BF16)
    lane = lax.broadcasted_iota(jnp.int32, lr_ref.shape, 1)
    lr_ref[...] = jnp.where(lane < GATE_RANK, w_ref[:, _LR0:_LR0 + LANE], 0.0).astype(BF16)


def _wprep(w_in, tr):
    layers = w_in.shape[0]

    def blk(n):
        return pl.BlockSpec((None, tr, n), lambda l, r: (l, r, 0))

    return pl.pallas_call(
        _wprep_body,
        out_shape=[jax.ShapeDtypeStruct((layers, D_MODEL, D_IN - GATE_RANK), BF16),
                   jax.ShapeDtypeStruct((layers, D_MODEL, LANE), BF16)],
        grid=(layers, D_MODEL // tr),
        in_specs=[blk(D_IN)],
        out_specs=[blk(D_IN - GATE_RANK), blk(LANE)],
        compiler_params=_cparams(2),
        name="wprep",
    )(w_in)

_OFF_Q, _OFF_K, _OFF_V, _OFF_G, _OFF_A, _OFF_GA, _OFF_MA, _OFF_MB, _OFF_END = (
    0, 512, 1024, 2048, 3072, 4096, 5120, 6144, 7168)


def _proj_body(h_ref, g_ref, wm_ref, wlr_ref, wdu_ref, bdec_ref,
               q_ref, k_ref, la_ref, v_ref, gs_ref, u_ref, sa_ref, sb_ref):
    xn = _rmsnorm(h_ref[...], g_ref[...]).astype(BF16)

    def mm(lo, hi):
        return _dot(xn, wm_ref[:, lo:hi])

    q_ref[...] = mm(_OFF_Q, _OFF_K) * (GLA_DK ** -0.5)
    k_ref[...] = mm(_OFF_K, _OFF_V)
    v_ref[...] = mm(_OFF_V, _OFF_G).astype(BF16)
    gs_ref[...] = _silu(mm(_OFF_G, _OFF_A)).astype(BF16)
    u_ref[...] = mm(_OFF_A, _OFF_GA) * _sigmoid(mm(_OFF_GA, _OFF_MA))
    sa_ref[...] = _sigmoid(mm(_OFF_MA, _OFF_MB)).astype(BF16)
    sb_ref[...] = _sigmoid(mm(_OFF_MB, _OFF_END)).astype(BF16)
    a_lr = _dot(xn, wlr_ref[...]).astype(BF16)
    z = _dot(a_lr, wdu_ref[...]) + bdec_ref[...]
    la_ref[...] = _log_sigmoid(z) / GATE_TAU


def _proj(h, w, layer, tm):
    rows = h.shape[0]

    def rs(n):
        return pl.BlockSpec((tm, n), lambda i: (i, 0))

    outs = ((GLA_KEY, F32), (GLA_KEY, F32), (GLA_KEY, F32), (GLA_VAL, BF16), (GLA_VAL, BF16),
            (D_MODEL, F32), (D_MODEL, BF16), (D_MODEL, BF16))
    return pl.pallas_call(
        _proj_body,
        out_shape=[jax.ShapeDtypeStruct((rows, n), dt) for n, dt in outs],
        grid=(rows // tm,),
        in_specs=[rs(D_MODEL), _layer_spec((1, D_MODEL), layer), _layer_spec((D_MODEL, _OFF_END), layer),
                  _layer_spec((D_MODEL, LANE), layer), _layer_spec((LANE, GLA_KEY), layer),
                  _layer_spec((1, GLA_KEY), layer)],
        out_specs=[rs(n) for n, _ in outs],
        compiler_params=_cparams(1),
        name="proj",
    )(h, *w)


def _chunk_cumsum(x, chunk):
    pos = lax.broadcasted_iota(jnp.int32, x.shape, 0) % chunk
    shift = 1
    while shift < chunk:
        x = x + jnp.where(pos >= shift, pltpu.roll(x, shift, 0), 0.0)
        shift *= 2
    return x


def _head_norm_gate(o, gnorm, gs):
    return o * lax.rsqrt(jnp.mean(o * o, axis=-1, keepdims=True) + EPS) * gnorm * gs


def _gla_blocks(q, k, v, la, gs, gnorm, states, blk, store):
    n_blk = q.shape[0] // blk
    heads = range(GLA_HEADS)
    kcs = [slice(h * GLA_DK, (h + 1) * GLA_DK) for h in heads]
    vcs = [slice(h * GLA_DV, (h + 1) * GLA_DV) for h in heads]
    carried = not isinstance(states[0], list)
    b_all = _chunk_cumsum(la, blk)
    q_t, k_t, k_u, v_b, dec = [], [], [], [], []
    for j in range(n_blk):
        rows = slice(j * blk, (j + 1) * blk)
        b = b_all[rows, :]
        b_last = b[blk - 1:blk, :]
        kj = k[rows, :]
        q_t.append((q[rows, :] * jnp.exp(b)).astype(BF16))
        k_t.append((kj * jnp.exp(-b)).astype(BF16))
        k_u.append((kj * jnp.exp(b_last - b)).astype(BF16))
        v_b.append(v[rows, :])
        dec.append(jnp.exp(b_last))
    tri = lax.broadcasted_iota(jnp.int32, (blk, blk), 0) >= lax.broadcasted_iota(jnp.int32, (blk, blk), 1)
    eye = (lax.broadcasted_iota(jnp.int32, (GLA_DK, GLA_DK), 0)
           == lax.broadcasted_iota(jnp.int32, (GLA_DK, GLA_DK), 1))
    probs = [[jnp.where(tri, lax.dot_general(q_t[j][:, kcs[h]], k_t[j][:, kcs[h]], NT_DIMS,
                                            preferred_element_type=F32), 0.0).astype(BF16)
              for h in heads] for j in range(n_blk)]
    upd = [[lax.dot_general(k_u[j][:, kcs[h]], v_b[j][:, vcs[h]], TN_DIMS, preferred_element_type=F32)
            for h in heads] for j in range(n_blk)]
    dec_col = [[jnp.sum(jnp.where(eye, jnp.broadcast_to(dec[j][:, kcs[h]], (GLA_DK, GLA_DK)), 0.0),
                        axis=1, keepdims=True) for h in heads] for j in range(n_blk)]
    finals = []
    cur = list(states) if carried else None
    for j in range(n_blk):
        rows = slice(j * blk, (j + 1) * blk)
        if not carried:
            cur = list(states[j])
        for h in heads:
            o = _dot(probs[j][h], v_b[j][:, vcs[h]]) + _dot(q_t[j][:, kcs[h]], cur[h].astype(BF16))
            cur[h] = dec_col[j][h] * cur[h] + upd[j][h]
            store(j, h, _head_norm_gate(o, gnorm[:, vcs[h]], gs[rows, vcs[h]]))
        finals.append(list(cur))
    return finals


def _gla_seq_body(q_ref, k_ref, la_ref, v_ref, gs_ref, s0_ref, gn_ref, o_ref, sfin_ref, s_scr, *, chunk):
    t = pl.program_id(1)

    @pl.when(t == 0)
    def _():
        s_scr[...] = s0_ref[...]

    def store(j, h, o):
        o_ref[j * chunk:(j + 1) * chunk, h * GLA_DV:(h + 1) * GLA_DV] = o.astype(o_ref.dtype)

    finals = _gla_blocks(q_ref[...], k_ref[...], v_ref[...], la_ref[...], gs_ref[...].astype(F32), gn_ref[...],
                         [s_scr[h] for h in range(GLA_HEADS)], chunk, store)
    for h in range(GLA_HEADS):
        s_scr[h] = finals[-1][h]

    @pl.when(t == pl.num_programs(1) - 1)
    def _():
        sfin_ref[...] = s_scr[...]


def _gla_seq(q, k, la, v, gs, s0, gnorm, layer, tl, chunk):
    bsz, seq, _ = q.shape

    def ts(n):
        return pl.BlockSpec((None, tl, n), lambda b, t: (b, t, 0))

    return pl.pallas_call(
        functools.partial(_gla_seq_body, chunk=chunk),
        out_shape=[jax.ShapeDtypeStruct((bsz, seq, GLA_VAL), BF16),
                   jax.ShapeDtypeStruct((bsz,) + STATE_TAIL, F32)],
        grid=(bsz, seq // tl),
        in_specs=[ts(GLA_KEY), ts(GLA_KEY), ts(GLA_KEY), ts(GLA_VAL), ts(GLA_VAL),
                  _const_spec(STATE_TAIL), _layer_spec((1, GLA_VAL), layer)],
        out_specs=[ts(GLA_VAL), pl.BlockSpec((None,) + STATE_TAIL, lambda b, t: (b, 0, 0, 0))],
        scratch_shapes=[pltpu.VMEM(STATE_TAIL, F32)],
        compiler_params=_cparams(2),
        name="gla_seq",
    )(q, k, la, v, gs, s0, gnorm)


def _gla_step_body(q_ref, k_ref, la_ref, v_ref, gs_ref, s0_ref, gn_ref, *rest, sb, seq):
    o_ref, sfin_ref = rest[-2:]

    def store(j, h, o):
        o_ref[j * seq:(j + 1) * seq, h * GLA_DV:(h + 1) * GLA_DV] = o.astype(o_ref.dtype)

    states = [[s0_ref[i, h] for h in range(GLA_HEADS)] for i in range(sb)]
    finals = _gla_blocks(q_ref[...], k_ref[...], v_ref[...], la_ref[...], gs_ref[...].astype(F32), gn_ref[...],
                         states, seq, store)
    for i in range(sb):
        for h in range(GLA_HEADS):
            sfin_ref[i, h] = finals[i][h]


def _gla_step(q, k, la, v, gs, s0_all, gnorm, layer, state_layer, seq, sb, row0, n_seq, prev_og, prev_state):
    rows = q.shape[0]
    blk = sb * seq
    base = row0 // blk

    def bs(n):
        return pl.BlockSpec((blk, n), lambda i: (base + i, 0))

    st = pl.BlockSpec((None, sb) + STATE_TAIL, lambda i: (state_layer, i, 0, 0, 0))
    in_specs = [bs(GLA_KEY), bs(GLA_KEY), bs(GLA_KEY), bs(GLA_VAL), bs(GLA_VAL), st, _layer_spec((1, GLA_VAL), layer)]
    args = [q, k, la, v, gs, s0_all, gnorm]
    aliases = _alias_prev(in_specs, args, prev_og, 0)
    aliases.update(_alias_prev(in_specs, args, prev_state, 1))
    return pl.pallas_call(
        functools.partial(_gla_step_body, sb=sb, seq=seq),
        out_shape=[jax.ShapeDtypeStruct((rows, GLA_VAL), F32), jax.ShapeDtypeStruct(s0_all.shape, F32)],
        grid=(n_seq // sb,),
        in_specs=in_specs,
        out_specs=[bs(GLA_VAL), st],
        input_output_aliases=aliases,
        compiler_params=_cparams(1),
        name="gla_step",
    )(*args)


def _conv_act(acc, cn):
    return _silu(_rmsnorm(acc, cn))


def _conv_step_body(u_ref, cache_ref, wdw_ref, bdw_ref, cn_ref, *rest, sb, seq):
    c_ref, ncache_ref = rest[-2:]
    u_t = jnp.swapaxes(u_ref[...].reshape(sb, seq, D_MODEL), 0, 1)

    def up(m):
        return cache_ref[m] if m < CONV_HIST else u_t[m - CONV_HIST]

    outs = []
    for t in range(seq):
        acc = jnp.broadcast_to(bdw_ref[...], (sb, D_MODEL))
        for j in range(CONV_WIDTH):
            acc = acc + wdw_ref[j:j + 1, :] * up(t + j)
        outs.append(_conv_act(acc, cn_ref[...]))
    c_ref[...] = jnp.swapaxes(jnp.stack(outs), 0, 1).reshape(sb * seq, D_MODEL)
    for m in range(CONV_HIST):
        ncache_ref[m] = up(m + seq)


def _conv_step(u, cache_all, conv_w, layer, cache_layer, seq, sb, row0, n_seq, prev_c, prev_cache):
    rows = u.shape[0]
    blk = sb * seq
    base = row0 // blk
    us = pl.BlockSpec((blk, D_MODEL), lambda i: (base + i, 0))
    cs = pl.BlockSpec((None, CONV_HIST, sb, D_MODEL), lambda i: (cache_layer, 0, i, 0))
    in_specs = [us, cs, _layer_spec((CONV_WIDTH, D_MODEL), layer), _layer_spec((1, D_MODEL), layer),
                _layer_spec((1, D_MODEL), layer)]
    args = [u, cache_all, *conv_w]
    aliases = _alias_prev(in_specs, args, prev_c, 0)
    aliases.update(_alias_prev(in_specs, args, prev_cache, 1))
    return pl.pallas_call(
        functools.partial(_conv_step_body, sb=sb, seq=seq),
        out_shape=[jax.ShapeDtypeStruct((rows, D_MODEL), F32), jax.ShapeDtypeStruct(cache_all.shape, F32)],
        grid=(n_seq // sb,),
        in_specs=in_specs,
        out_specs=[us, cs],
        input_output_aliases=aliases,
        compiler_params=_cparams(1),
        name="conv_step",
    )(*args)


def _conv_piece(ubuf, wdw_ref, r0, l0):
    pad = HIST_ROWS - CONV_HIST
    xs = ubuf[r0:r0 + CONV_RB + HIST_ROWS, l0:l0 + LANE]
    out = None
    for s in range(8):
        rows = CONV_RB if s == 0 else CONV_RB + 8
        z = None
        for a in range(HIST_ROWS // 8 + 1):
            j = 8 * a + s - pad
            if 0 <= j < CONV_WIDTH:
                term = wdw_ref[j:j + 1, l0:l0 + LANE] * xs[8 * a:8 * a + rows]
                z = term if z is None else z + term
        z = z[s:s + CONV_RB]
        out = z if out is None else out + z
    return out


def _merge_math(h, c_act, og, sa, sb, wpw_ref, wout_ref):
    o_conv = _dot(c_act, wpw_ref[...])
    merged = (sa * og + sb * o_conv).astype(BF16)
    return h + _dot(merged, wout_ref[...])


def _merge_body(h_ref, c_ref, og_ref, sa_ref, sb_ref, wpw_ref, wout_ref, o_ref):
    o_ref[...] = _merge_math(h_ref[...], c_ref[...].astype(BF16), og_ref[...], sa_ref[...].astype(F32),
                             sb_ref[...].astype(F32), wpw_ref, wout_ref)


def _merge(h, c, og, sa, sb, merge_w, layer, tm):
    rows = h.shape[0]
    row_spec = pl.BlockSpec((tm, D_MODEL), lambda i: (i, 0))
    sq = _layer_spec((D_MODEL, D_MODEL), layer)
    return pl.pallas_call(
        _merge_body,
        out_shape=jax.ShapeDtypeStruct((rows, D_MODEL), F32),
        grid=(rows // tm,),
        in_specs=[row_spec] * 5 + [sq, sq],
        out_specs=row_spec,
        compiler_params=_cparams(1),
        name="merge",
    )(h, c, og, sa, sb, *merge_w)


def _conv_merge_body(u_ref, cache_ref, h_ref, og_ref, sa_ref, sb_ref, wdw_ref, bdw_ref, cn_ref, wpw_ref, wout_ref,
                     o_ref, ncache_ref, ubuf, cbuf, *, tl):
    t = pl.program_id(1)
    pad = HIST_ROWS - CONV_HIST

    @pl.when(t == 0)
    def _():
        ubuf[0:pad, :] = jnp.zeros((pad, D_MODEL), F32)
        ubuf[pad:HIST_ROWS, :] = cache_ref[...]

    @pl.when(t > 0)
    def _():
        ubuf[0:HIST_ROWS, :] = ubuf[tl:tl + HIST_ROWS, :]

    ubuf[HIST_ROWS:HIST_ROWS + tl, :] = u_ref[...]
    for rb in range(tl // CONV_RB):
        rows = slice(rb * CONV_RB, (rb + 1) * CONV_RB)
        for lc in range(D_MODEL // LANE):
            lanes = slice(lc * LANE, (lc + 1) * LANE)
            cbuf[rows, lanes] = _conv_piece(ubuf, wdw_ref, rb * CONV_RB, lc * LANE) + bdw_ref[:, lanes]
    c_act = _conv_act(cbuf[...], cn_ref[...]).astype(BF16)
    o_ref[...] = _merge_math(h_ref[...], c_act, og_ref[...].astype(F32), sa_ref[...].astype(F32),
                             sb_ref[...].astype(F32), wpw_ref, wout_ref)

    @pl.when(t == pl.num_programs(1) - 1)
    def _():
        ncache_ref[...] = ubuf[tl + pad:tl + HIST_ROWS, :]


def _conv_merge(u, cache, h, og, sa, sb, conv_w, merge_w, layer, tl):
    bsz, seq, _ = h.shape
    tile = pl.BlockSpec((None, tl, D_MODEL), lambda b, t: (b, t, 0))
    vec = _layer_spec((1, D_MODEL), layer)
    sq = _layer_spec((D_MODEL, D_MODEL), layer)
    return pl.pallas_call(
        functools.partial(_conv_merge_body, tl=tl),
        out_shape=[jax.ShapeDtypeStruct((bsz, seq, D_MODEL), F32),
                   jax.ShapeDtypeStruct((bsz, CONV_HIST, D_MODEL), F32)],
        grid=(bsz, seq // tl),
        in_specs=[tile, _const_spec((CONV_HIST, D_MODEL)), tile, tile, tile, tile,
                  _layer_spec((CONV_WIDTH, D_MODEL), layer), vec, vec, sq, sq],
        out_specs=[tile, pl.BlockSpec((None, CONV_HIST, D_MODEL), lambda b, t: (b, 0, 0))],
        scratch_shapes=[pltpu.VMEM((tl + HIST_ROWS, D_MODEL), F32), pltpu.VMEM((tl, D_MODEL), F32)],
        compiler_params=_cparams(2),
        name="conv_merge",
    )(u, cache, h, og, sa, sb, *conv_w, *merge_w)


def kernel(x_prompt, x_sample, state_gla, cache_conv, meta_tokens, norm_ffn1, w_ffn1_gate, w_ffn1_up, w_ffn1_down, norm_mix, w_in, w_decay_up, b_decay, gla_norm, w_dw, b_dw, conv_norm, w_pw, w_out, norm_ffn2, w_ffn2_gate, w_ffn2_up, w_ffn2_down, norm_final):
    n_prompt, l_prompt, _ = x_prompt.shape
    n_sample, l_sample, _ = x_sample.shape
    rows_p = n_prompt * l_prompt
    rows_s = n_sample * l_sample
    rows_small = rows_s + N_META

    hp = x_prompt.reshape(rows_p, D_MODEL)
    hs = jnp.concatenate([x_sample.reshape(rows_s, D_MODEL), meta_tokens], axis=0)

    tm_ffn_p, tm_proj_p, tl_prompt = 1024, 512, 512
    tm_proj_s = 208
    sb_gla, sb_conv = 8, 32

    def vec(a):
        return a.reshape(a.shape[0], 1, a.shape[-1])

    def bf(a):
        return a.astype(BF16)

    w_main, w_lr = _wprep(w_in, 256)
    w_du = jnp.pad(w_decay_up, ((0, 0), (0, LANE - GATE_RANK), (0, 0))).astype(BF16)

    ffn1_w = (vec(norm_ffn1), bf(w_ffn1_gate), bf(w_ffn1_up), bf(w_ffn1_down))
    ffn2_w = (vec(norm_ffn2), bf(w_ffn2_gate), bf(w_ffn2_up), bf(w_ffn2_down))
    proj_w = (vec(norm_mix), w_main, w_lr, w_du, vec(b_decay))
    conv_w = (w_dw, vec(b_dw), vec(conv_norm))
    merge_w = (bf(w_pw), bf(w_out))
    gnorm = vec(gla_norm)
    gfin = norm_final.reshape(1, D_MODEL)

    zero_state = jnp.zeros((1, 1) + STATE_TAIL, F32)
    zero_cache = jnp.zeros((1, CONV_HIST, 1, D_MODEL), F32)
    cache_tm = jnp.swapaxes(cache_conv, 1, 2)

    def pseq(a):
        return a.reshape(n_prompt, l_prompt, a.shape[-1])

    sp_list, cp_list = [], []
    state_s = cache_s = None
    for l in range(DEPTH):
        last = l == DEPTH - 1
        hp = _ffn(hp, ffn1_w, l, tm_ffn_p)
        hs = _ffn_cols(hs, ffn1_w, l)

        q, k, la, v, gs, u, sa, sb = _proj(hs, proj_w, l, tm_proj_s)
        og, state_s = _gla_step(q, k, la, v, gs, state_gla, gnorm, l, l, l_sample, sb_gla, 0, n_sample,
                                None, state_s)
        og, s_m = _gla_step(q, k, la, v, gs, zero_state, gnorm, l, 0, N_META, 1, rows_s, 1, og, None)
        cc, cache_s = _conv_step(u, cache_tm, conv_w, l, l, l_sample, sb_conv, 0, n_sample, None, cache_s)
        cc, nc_m = _conv_step(u, zero_cache, conv_w, l, 0, N_META, 1, rows_s, 1, cc, None)
        hs = _merge(hs, cc, og, sa, sb, merge_w, l, rows_small)
        hs = _ffn_cols(hs, ffn2_w, l, gfin if last else None)

        q, k, la, v, gs, u, sa, sb = _proj(hp, proj_w, l, tm_proj_p)
        og_p, s_p = _gla_seq(pseq(q), pseq(k), pseq(la), pseq(v), pseq(gs), s_m[0, 0], gnorm, l, tl_prompt, CHUNK)
        hp, nc_p = _conv_merge(pseq(u), nc_m[0, :, 0, :], pseq(hp), og_p, pseq(sa), pseq(sb), conv_w, merge_w, l,
                               tl_prompt)
        hp = _ffn(hp.reshape(rows_p, D_MODEL), ffn2_w, l, tm_ffn_p, gfin if last else None)
        sp_list.append(s_p)
        cp_list.append(nc_p)

    y_prompt = hp.reshape(n_prompt, l_prompt, D_MODEL)
    y_sample = hs[:rows_s].reshape(n_sample, l_sample, D_MODEL)
    return (y_prompt, y_sample, jnp.stack(sp_list), jnp.stack(cp_list), state_s, jnp.swapaxes(cache_s, 1, 2))
```

```python
import functools

import jax
import jax.numpy as jnp
from jax import lax
from jax.experimental import pallas as pl
from jax.experimental.pallas import tpu as pltpu

F32 = jnp.float32
BF16 = jnp.bfloat16

D_MODEL = 1024
DEPTH = 4
N_META = 16
GLA_HEADS = 4
GLA_DK = 128
GLA_DV = 256
GLA_KEY = GLA_HEADS * GLA_DK
GLA_VAL = GLA_HEADS * GLA_DV
GATE_RANK = 16
GATE_TAU = 16.0
CHUNK = 64
CONV_WIDTH = 31
CONV_HIST = CONV_WIDTH - 1
D_FF = 2816
EPS = 1e-6

LANE = 128
FF_COLS = 256
HIST_ROWS = 32
CONV_RB = 128
VMEM_LIMIT = 56 * 1024 * 1024

STATE_TAIL = (GLA_HEADS, GLA_DK, GLA_DV)
NT_DIMS = (((1,), (1,)), ((), ()))
TN_DIMS = (((0,), (0,)), ((), ()))


def _cparams(n_axes):
    return pltpu.CompilerParams(dimension_semantics=("arbitrary",) * n_axes, vmem_limit_bytes=VMEM_LIMIT)


def _const_spec(shape):
    nd = len(shape)
    return pl.BlockSpec(shape, lambda *_: (0,) * nd, pipeline_mode=pl.Buffered(1))


def _layer_spec(shape, layer):
    nd = len(shape)
    return pl.BlockSpec((None,) + tuple(shape), lambda *_: (layer,) + (0,) * nd, pipeline_mode=pl.Buffered(1))


def _alias_prev(in_specs, args, prev_out, out_index):
    if prev_out is None:
        return {}
    in_specs.append(pl.BlockSpec(memory_space=pl.ANY))
    args.append(prev_out)
    return {len(args) - 1: out_index}


def _rmsnorm(x, g):
    return x * lax.rsqrt(jnp.mean(x * x, axis=-1, keepdims=True) + EPS) * g


def _sigmoid(x):
    return 1.0 / (1.0 + jnp.exp(-x))


def _silu(x):
    return x * _sigmoid(x)


def _log_sigmoid(x):
    return jnp.minimum(x, 0.0) - jnp.log1p(jnp.exp(-jnp.abs(x)))


def _dot(a, b):
    return jnp.dot(a, b, preferred_element_type=F32)


def _ffn_math(x, g, wg_ref, wu_ref, wd_ref):
    xn = _rmsnorm(x, g).astype(BF16)
    acc = None
    for c in range(D_FF // FF_COLS):
        cols = slice(c * FF_COLS, (c + 1) * FF_COLS)
        gate = _dot(xn, wg_ref[:, cols])
        up = _dot(xn, wu_ref[:, cols])
        act = (_silu(gate) * up).astype(BF16)
        part = _dot(act, wd_ref[cols, :])
        acc = part if acc is None else acc + part
    return x + 0.5 * acc


def _ffn_body(h_ref, g_ref, wg_ref, wu_ref, wd_ref, *rest):
    o_ref = rest[-1]
    y = _ffn_math(h_ref[...], g_ref[...], wg_ref, wu_ref, wd_ref)
    if len(rest) == 2:
        y = _rmsnorm(y, rest[0][...])
    o_ref[...] = y


def _ffn(h, w, layer, tm, final_gain=None):
    rows = h.shape[0]
    row_spec = pl.BlockSpec((tm, D_MODEL), lambda i: (i, 0))
    in_specs = [row_spec, _layer_spec((1, D_MODEL), layer), _layer_spec((D_MODEL, D_FF), layer),
                _layer_spec((D_MODEL, D_FF), layer), _layer_spec((D_FF, D_MODEL), layer)]
    args = [h, *w]
    if final_gain is not None:
        in_specs.append(_const_spec((1, D_MODEL)))
        args.append(final_gain)
    return pl.pallas_call(
        _ffn_body,
        out_shape=jax.ShapeDtypeStruct((rows, D_MODEL), F32),
        grid=(rows // tm,),
        in_specs=in_specs,
        out_specs=row_spec,
        compiler_params=_cparams(1),
        name="ffn",
    )(*args)


_OFF_Q, _OFF_K, _OFF_V, _OFF_G, _OFF_LR, _OFF_A, _OFF_GA, _OFF_MA, _OFF_MB, D_IN = (
    0, 512, 1024, 2048, 3072, 3088, 4112, 5136, 6160, 7184)


def _proj_body(h_ref, g_ref, wt_ref, wdu_ref, bdec_ref,
               q_ref, k_ref, la_ref, v_ref, gs_ref, u_ref, sa_ref, sb_ref):
    xn = _rmsnorm(h_ref[...], g_ref[...]).astype(BF16)

    def mm(lo, hi):
        return lax.dot_general(xn, wt_ref[lo:hi, :], NT_DIMS, preferred_element_type=F32)

    q_ref[...] = mm(_OFF_Q, _OFF_K) * (GLA_DK ** -0.5)
    k_ref[...] = mm(_OFF_K, _OFF_V)
    v_ref[...] = mm(_OFF_V, _OFF_G).astype(BF16)
    gs_ref[...] = _silu(mm(_OFF_G, _OFF_LR)).astype(BF16)
    u_ref[...] = mm(_OFF_A, _OFF_GA) * _sigmoid(mm(_OFF_GA, _OFF_MA))
    sa_ref[...] = _sigmoid(mm(_OFF_MA, _OFF_MB)).astype(BF16)
    sb_ref[...] = _sigmoid(mm(_OFF_MB, D_IN)).astype(BF16)
    lane = lax.broadcasted_iota(jnp.int32, (xn.shape[0], LANE), 1)
    a_lr = jnp.where(lane < GATE_RANK, mm(_OFF_LR, _OFF_LR + LANE), 0.0).astype(BF16)
    z = _dot(a_lr, wdu_ref[...]) + bdec_ref[...]
    la_ref[...] = _log_sigmoid(z) / GATE_TAU


def _proj(h, w, layer, tm):
    rows = h.shape[0]

    def rs(n):
        return pl.BlockSpec((tm, n), lambda i: (i, 0))

    outs = ((GLA_KEY, F32), (GLA_KEY, F32), (GLA_KEY, F32), (GLA_VAL, BF16), (GLA_VAL, BF16),
            (D_MODEL, F32), (D_MODEL, BF16), (D_MODEL, BF16))
    return pl.pallas_call(
        _proj_body,
        out_shape=[jax.ShapeDtypeStruct((rows, n), dt) for n, dt in outs],
        grid=(rows // tm,),
        in_specs=[rs(D_MODEL), _layer_spec((1, D_MODEL), layer), _layer_spec((D_IN, D_MODEL), layer),
                  _layer_spec((LANE, GLA_KEY), layer), _layer_spec((1, GLA_KEY), layer)],
        out_specs=[rs(n) for n, _ in outs],
        compiler_params=_cparams(1),
        name="proj",
    )(h, *w)


def _chunk_cumsum(x, chunk):
    pos = lax.broadcasted_iota(jnp.int32, x.shape, 0) % chunk
    shift = 1
    while shift < chunk:
        x = x + jnp.where(pos >= shift, pltpu.roll(x, shift, 0), 0.0)
        shift *= 2
    return x


def _head_norm_gate(o, gnorm, gs):
    return o * lax.rsqrt(jnp.mean(o * o, axis=-1, keepdims=True) + EPS) * gnorm * gs


def _gla_blocks(q, k, v, la, gs, gnorm, states, blk, store):
    n_blk = q.shape[0] // blk
    heads = range(GLA_HEADS)
    kcs = [slice(h * GLA_DK, (h + 1) * GLA_DK) for h in heads]
    vcs = [slice(h * GLA_DV, (h + 1) * GLA_DV) for h in heads]
    carried = not isinstance(states[0], list)
    b_all = _chunk_cumsum(la, blk)
    q_t, k_t, k_u, v_b, dec = [], [], [], [], []
    for j in range(n_blk):
        rows = slice(j * blk, (j + 1) * blk)
        b = b_all[rows, :]
        b_last = b[blk - 1:blk, :]
        kj = k[rows, :]
        q_t.append((q[rows, :] * jnp.exp(b)).astype(BF16))
        k_t.append((kj * jnp.exp(-b)).astype(BF16))
        k_u.append((kj * jnp.exp(b_last - b)).astype(BF16))
        v_b.append(v[rows, :])
        dec.append(jnp.exp(b_last))
    tri = lax.broadcasted_iota(jnp.int32, (blk, blk), 0) >= lax.broadcasted_iota(jnp.int32, (blk, blk), 1)
    eye = (lax.broadcasted_iota(jnp.int32, (GLA_DK, GLA_DK), 0)
           == lax.broadcasted_iota(jnp.int32, (GLA_DK, GLA_DK), 1))
    probs = [[jnp.where(tri, lax.dot_general(q_t[j][:, kcs[h]], k_t[j][:, kcs[h]], NT_DIMS,
                                            preferred_element_type=F32), 0.0).astype(BF16)
              for h in heads] for j in range(n_blk)]
    upd = [[lax.dot_general(k_u[j][:, kcs[h]], v_b[j][:, vcs[h]], TN_DIMS, preferred_element_type=F32)
            for h in heads] for j in range(n_blk)]
    dec_col = [[jnp.sum(jnp.where(eye, jnp.broadcast_to(dec[j][:, kcs[h]], (GLA_DK, GLA_DK)), 0.0),
                        axis=1, keepdims=True) for h in heads] for j in range(n_blk)]
    finals = []
    cur = list(states) if carried else None
    for j in range(n_blk):
        rows = slice(j * blk, (j + 1) * blk)
        if not carried:
            cur = list(states[j])
        for h in heads:
            o = _dot(probs[j][h], v_b[j][:, vcs[h]]) + _dot(q_t[j][:, kcs[h]], cur[h].astype(BF16))
            cur[h] = dec_col[j][h] * cur[h] + upd[j][h]
            store(j, h, _head_norm_gate(o, gnorm[:, vcs[h]], gs[rows, vcs[h]]))
        finals.append(list(cur))
    return finals


def _gla_step_body(q_ref, k_ref, la_ref, v_ref, gs_ref, s0_ref, gn_ref, *rest, sb, seq):
    o_ref, sfin_ref = rest[-2:]

    def store(j, h, o):
        o_ref[j * seq:(j + 1) * seq, h * GLA_DV:(h + 1) * GLA_DV] = o.astype(o_ref.dtype)

    states = [[s0_ref[i, h] for h in range(GLA_HEADS)] for i in range(sb)]
    finals = _gla_blocks(q_ref[...], k_ref[...], v_ref[...], la_ref[...], gs_ref[...].astype(F32), gn_ref[...],
                         states, seq, store)
    for i in range(sb):
        for h in range(GLA_HEADS):
            sfin_ref[i, h] = finals[i][h]


def _gla_step(q, k, la, v, gs, s0_all, gnorm, layer, state_layer, seq, sb, row0, n_seq, prev_og, prev_state):
    rows = q.shape[0]
    blk = sb * seq
    base = row0 // blk

    def bs(n):
        return pl.BlockSpec((blk, n), lambda i: (base + i, 0))

    st = pl.BlockSpec((None, sb) + STATE_TAIL, lambda i: (state_layer, i, 0, 0, 0))
    in_specs = [bs(GLA_KEY), bs(GLA_KEY), bs(GLA_KEY), bs(GLA_VAL), bs(GLA_VAL), st, _layer_spec((1, GLA_VAL), layer)]
    args = [q, k, la, v, gs, s0_all, gnorm]
    aliases = _alias_prev(in_specs, args, prev_og, 0)
    aliases.update(_alias_prev(in_specs, args, prev_state, 1))
    return pl.pallas_call(
        functools.partial(_gla_step_body, sb=sb, seq=seq),
        out_shape=[jax.ShapeDtypeStruct((rows, GLA_VAL), F32), jax.ShapeDtypeStruct(s0_all.shape, F32)],
        grid=(n_seq // sb,),
        in_specs=in_specs,
        out_specs=[bs(GLA_VAL), st],
        input_output_aliases=aliases,
        compiler_params=_cparams(1),
        name="gla_step",
    )(*args)


def _conv_act(acc, cn):
    return _silu(_rmsnorm(acc, cn))


def _conv_step_body(u_ref, cache_ref, wdw_ref, bdw_ref, cn_ref, *rest, sb, seq):
    c_ref, ncache_ref = rest[-2:]
    u_t = jnp.swapaxes(u_ref[...].reshape(sb, seq, D_MODEL), 0, 1)

    def up(m):
        return cache_ref[m] if m < CONV_HIST else u_t[m - CONV_HIST]

    outs = []
    for t in range(seq):
        acc = jnp.broadcast_to(bdw_ref[...], (sb, D_MODEL))
        for j in range(CONV_WIDTH):
            acc = acc + wdw_ref[j:j + 1, :] * up(t + j)
        outs.append(_conv_act(acc, cn_ref[...]))
    c_ref[...] = jnp.swapaxes(jnp.stack(outs), 0, 1).reshape(sb * seq, D_MODEL)
    for m in range(CONV_HIST):
        ncache_ref[m] = up(m + seq)


def _conv_step(u, cache_all, conv_w, layer, cache_layer, seq, sb, row0, n_seq, prev_c, prev_cache):
    rows = u.shape[0]
    blk = sb * seq
    base = row0 // blk
    us = pl.BlockSpec((blk, D_MODEL), lambda i: (base + i, 0))
    cs = pl.BlockSpec((None, CONV_HIST, sb, D_MODEL), lambda i: (cache_layer, 0, i, 0))
    in_specs = [us, cs, _layer_spec((CONV_WIDTH, D_MODEL), layer), _layer_spec((1, D_MODEL), layer),
                _layer_spec((1, D_MODEL), layer)]
    args = [u, cache_all, *conv_w]
    aliases = _alias_prev(in_specs, args, prev_c, 0)
    aliases.update(_alias_prev(in_specs, args, prev_cache, 1))
    return pl.pallas_call(
        functools.partial(_conv_step_body, sb=sb, seq=seq),
        out_shape=[jax.ShapeDtypeStruct((rows, D_MODEL), F32), jax.ShapeDtypeStruct(cache_all.shape, F32)],
        grid=(n_seq // sb,),
        in_specs=in_specs,
        out_specs=[us, cs],
        input_output_aliases=aliases,
        compiler_params=_cparams(1),
        name="conv_step",
    )(*args)


def _conv_piece(ubuf, wdw_ref, r0, l0):
    pad = HIST_ROWS - CONV_HIST
    xs = ubuf[r0:r0 + CONV_RB + HIST_ROWS, l0:l0 + LANE]
    out = None
    for s in range(8):
        rows = CONV_RB if s == 0 else CONV_RB + 8
        z = None
        for a in range(HIST_ROWS // 8 + 1):
            j = 8 * a + s - pad
            if 0 <= j < CONV_WIDTH:
                term = wdw_ref[j:j + 1, l0:l0 + LANE] * xs[8 * a:8 * a + rows]
                z = term if z is None else z + term
        z = z[s:s + CONV_RB]
        out = z if out is None else out + z
    return out


def _merge_math(h, c_act, og, sa, sb, wpw_ref, wout_ref):
    o_conv = _dot(c_act, wpw_ref[...])
    merged = (sa * og + sb * o_conv).astype(BF16)
    return h + _dot(merged, wout_ref[...])


def _merge_body(h_ref, c_ref, og_ref, sa_ref, sb_ref, wpw_ref, wout_ref, o_ref):
    o_ref[...] = _merge_math(h_ref[...], c_ref[...].astype(BF16), og_ref[...], sa_ref[...].astype(F32),
                             sb_ref[...].astype(F32), wpw_ref, wout_ref)


def _merge(h, c, og, sa, sb, merge_w, layer, tm):
    rows = h.shape[0]
    row_spec = pl.BlockSpec((tm, D_MODEL), lambda i: (i, 0))
    sq = _layer_spec((D_MODEL, D_MODEL), layer)
    return pl.pallas_call(
        _merge_body,
        out_shape=jax.ShapeDtypeStruct((rows, D_MODEL), F32),
        grid=(rows // tm,),
        in_specs=[row_spec] * 5 + [sq, sq],
        out_specs=row_spec,
        compiler_params=_cparams(1),
        name="merge",
    )(h, c, og, sa, sb, *merge_w)


def _mixer_seq_body(q_ref, k_ref, la_ref, v_ref, gs_ref, s0_ref, gn_ref, u_ref, cache_ref, h_ref, sa_ref, sb_ref,
                    wdw_ref, bdw_ref, cn_ref, wpw_ref, wout_ref, o_ref, sfin_ref, ncache_ref,
                    s_scr, og_scr, ubuf, cbuf, *, tl, chunk):
    t = pl.program_id(1)
    pad = HIST_ROWS - CONV_HIST

    @pl.when(t == 0)
    def _():
        s_scr[...] = s0_ref[...]
        ubuf[0:pad, :] = jnp.zeros((pad, D_MODEL), F32)
        ubuf[pad:HIST_ROWS, :] = cache_ref[...]

    @pl.when(t > 0)
    def _():
        ubuf[0:HIST_ROWS, :] = ubuf[tl:tl + HIST_ROWS, :]

    def store(j, h, o):
        og_scr[j * chunk:(j + 1) * chunk, h * GLA_DV:(h + 1) * GLA_DV] = o

    finals = _gla_blocks(q_ref[...], k_ref[...], v_ref[...], la_ref[...], gs_ref[...].astype(F32), gn_ref[...],
                         [s_scr[h] for h in range(GLA_HEADS)], chunk, store)
    for h in range(GLA_HEADS):
        s_scr[h] = finals[-1][h]

    ubuf[HIST_ROWS:HIST_ROWS + tl, :] = u_ref[...]
    for rb in range(tl // CONV_RB):
        rows = slice(rb * CONV_RB, (rb + 1) * CONV_RB)
        for lc in range(D_MODEL // LANE):
            lanes = slice(lc * LANE, (lc + 1) * LANE)
            cbuf[rows, lanes] = _conv_piece(ubuf, wdw_ref, rb * CONV_RB, lc * LANE) + bdw_ref[:, lanes]
    c_act = _conv_act(cbuf[...], cn_ref[...]).astype(BF16)
    o_ref[...] = _merge_math(h_ref[...], c_act, og_scr[...], sa_ref[...].astype(F32), sb_ref[...].astype(F32),
                             wpw_ref, wout_ref)

    @pl.when(t == pl.num_programs(1) - 1)
    def _():
        sfin_ref[...] = s_scr[...]
        ncache_ref[...] = ubuf[tl + pad:tl + HIST_ROWS, :]


def _mixer_seq(q, k, la, v, gs, u, h, sa, sb, s0, cache, gnorm, conv_w, merge_w, layer, tl, chunk):
    bsz, seq, _ = h.shape

    def ts(n):
        return pl.BlockSpec((None, tl, n), lambda b, t: (b, t, 0))

    vec = _layer_spec((1, D_MODEL), layer)
    sq = _layer_spec((D_MODEL, D_MODEL), layer)
    return pl.pallas_call(
        functools.partial(_mixer_seq_body, tl=tl, chunk=chunk),
        out_shape=[jax.ShapeDtypeStruct((bsz, seq, D_MODEL), F32),
                   jax.ShapeDtypeStruct((bsz,) + STATE_TAIL, F32),
                   jax.ShapeDtypeStruct((bsz, CONV_HIST, D_MODEL), F32)],
        grid=(bsz, seq // tl),
        in_specs=[ts(GLA_KEY), ts(GLA_KEY), ts(GLA_KEY), ts(GLA_VAL), ts(GLA_VAL),
                  _const_spec(STATE_TAIL), _layer_spec((1, GLA_VAL), layer),
                  ts(D_MODEL), _const_spec((CONV_HIST, D_MODEL)), ts(D_MODEL), ts(D_MODEL), ts(D_MODEL),
                  _layer_spec((CONV_WIDTH, D_MODEL), layer), vec, vec, sq, sq],
        out_specs=[ts(D_MODEL), pl.BlockSpec((None,) + STATE_TAIL, lambda b, t: (b, 0, 0, 0)),
                   pl.BlockSpec((None, CONV_HIST, D_MODEL), lambda b, t: (b, 0, 0))],
        scratch_shapes=[pltpu.VMEM(STATE_TAIL, F32), pltpu.VMEM((tl, GLA_VAL), F32),
                        pltpu.VMEM((tl + HIST_ROWS, D_MODEL), F32), pltpu.VMEM((tl, D_MODEL), F32)],
        compiler_params=_cparams(2),
        name="mixer_seq",
    )(q, k, la, v, gs, s0, gnorm, u, cache, h, sa, sb, *conv_w, *merge_w)


def kernel(x_prompt, x_sample, state_gla, cache_conv, meta_tokens, norm_ffn1, w_ffn1_gate, w_ffn1_up, w_ffn1_down, norm_mix, w_in, w_decay_up, b_decay, gla_norm, w_dw, b_dw, conv_norm, w_pw, w_out, norm_ffn2, w_ffn2_gate, w_ffn2_up, w_ffn2_down, norm_final):
    n_prompt, l_prompt, _ = x_prompt.shape
    n_sample, l_sample, _ = x_sample.shape
    rows_p = n_prompt * l_prompt
    rows_s = n_sample * l_sample
    rows_small = rows_s + N_META

    hp = x_prompt.reshape(rows_p, D_MODEL)
    hs = jnp.concatenate([x_sample.reshape(rows_s, D_MODEL), meta_tokens], axis=0)

    tm_ffn_p, tm_proj_p, tl_prompt = 1024, 512, 512
    tm_proj_s = 208
    sb_gla, sb_conv = 8, 32

    def vec(a):
        return a.reshape(a.shape[0], 1, a.shape[-1])

    def bf(a):
        return a.astype(BF16)

    w_in_t = bf(jnp.swapaxes(w_in, 1, 2))
    w_du = jnp.pad(w_decay_up, ((0, 0), (0, LANE - GATE_RANK), (0, 0))).astype(BF16)

    ffn1_w = (vec(norm_ffn1), bf(w_ffn1_gate), bf(w_ffn1_up), bf(w_ffn1_down))
    ffn2_w = (vec(norm_ffn2), bf(w_ffn2_gate), bf(w_ffn2_up), bf(w_ffn2_down))
    proj_w = (vec(norm_mix), w_in_t, w_du, vec(b_decay))
    conv_w = (w_dw, vec(b_dw), vec(conv_norm))
    merge_w = (bf(w_pw), bf(w_out))
    gnorm = vec(gla_norm)
    gfin = norm_final.reshape(1, D_MODEL)

    zero_state = jnp.zeros((1, 1) + STATE_TAIL, F32)
    zero_cache = jnp.zeros((1, CONV_HIST, 1, D_MODEL), F32)
    cache_tm = jnp.swapaxes(cache_conv, 1, 2)

    def pseq(a):
        return a.reshape(n_prompt, l_prompt, a.shape[-1])

    sp_list, cp_list = [], []
    state_s = cache_s = None
    for l in range(DEPTH):
        last = l == DEPTH - 1
        hp = _ffn(hp, ffn1_w, l, tm_ffn_p)
        hs = _ffn(hs, ffn1_w, l, rows_small)

        q, k, la, v, gs, u, sa, sb = _proj(hs, proj_w, l, tm_proj_s)
        og, state_s = _gla_step(q, k, la, v, gs, state_gla, gnorm, l, l, l_sample, sb_gla, 0, n_sample,
                                None, state_s)
        og, s_m = _gla_step(q, k, la, v, gs, zero_state, gnorm, l, 0, N_META, 1, rows_s, 1, og, None)
        cc, cache_s = _conv_step(u, cache_tm, conv_w, l, l, l_sample, sb_conv, 0, n_sample, None, cache_s)
        cc, nc_m = _conv_step(u, zero_cache, conv_w, l, 0, N_META, 1, rows_s, 1, cc, None)
        hs = _merge(hs, cc, og, sa, sb, merge_w, l, rows_small)
        hs = _ffn(hs, ffn2_w, l, rows_small, gfin if last else None)

        q, k, la, v, gs, u, sa, sb = _proj(hp, proj_w, l, tm_proj_p)
        hp, s_p, nc_p = _mixer_seq(pseq(q), pseq(k), pseq(la), pseq(v), pseq(gs), pseq(u), pseq(hp), pseq(sa),
                                   pseq(sb), s_m[0, 0], nc_m[0, :, 0, :], gnorm, conv_w, merge_w, l, tl_prompt, CHUNK)
        hp = _ffn(hp.reshape(rows_p, D_MODEL), ffn2_w, l, tm_ffn_p, gfin if last else None)
        sp_list.append(s_p)
        cp_list.append(nc_p)

    y_prompt = hp.reshape(n_prompt, l_prompt, D_MODEL)
    y_sample = hs[:rows_s].reshape(n_sample, l_sample, D_MODEL)
    return (y_prompt, y_sample, jnp.stack(sp_list), jnp.stack(cp_list), state_s, jnp.swapaxes(cache_s, 1, 2))
```

```python
import functools

import jax
import jax.numpy as jnp
from jax import lax
from jax.experimental import pallas as pl
from jax.experimental.pallas import tpu as pltpu

F32 = jnp.float32
BF16 = jnp.bfloat16

D_MODEL = 1024
DEPTH = 4
N_META = 16
GLA_HEADS = 4
GLA_DK = 128
GLA_DV = 256
GLA_KEY = GLA_HEADS * GLA_DK
GLA_VAL = GLA_HEADS * GLA_DV
GATE_RANK = 16
GATE_TAU = 16.0
CHUNK = 64
CONV_WIDTH = 31
CONV_HIST = CONV_WIDTH - 1
D_FF = 2816
EPS = 1e-6

LANE = 128
FF_COLS = 256
HIST_ROWS = 32
CONV_RB = 128
VMEM_LIMIT = 56 * 1024 * 1024

STATE_TAIL = (GLA_HEADS, GLA_DK, GLA_DV)
NT_DIMS = (((1,), (1,)), ((), ()))
TN_DIMS = (((0,), (0,)), ((), ()))


def _cparams(n_axes):
    return pltpu.CompilerParams(dimension_semantics=("arbitrary",) * n_axes, vmem_limit_bytes=VMEM_LIMIT)


def _const_spec(shape):
    nd = len(shape)
    return pl.BlockSpec(shape, lambda *_: (0,) * nd, pipeline_mode=pl.Buffered(1))


def _layer_spec(shape, layer):
    nd = len(shape)
    return pl.BlockSpec((None,) + tuple(shape), lambda *_: (layer,) + (0,) * nd, pipeline_mode=pl.Buffered(1))


def _alias_prev(in_specs, args, prev_out, out_index):
    if prev_out is None:
        return {}
    in_specs.append(pl.BlockSpec(memory_space=pl.ANY))
    args.append(prev_out)
    return {len(args) - 1: out_index}


def _rmsnorm(x, g):
    return x * lax.rsqrt(jnp.mean(x * x, axis=-1, keepdims=True) + EPS) * g


def _sigmoid(x):
    return 1.0 / (1.0 + jnp.exp(-x))


def _silu(x):
    return x * _sigmoid(x)


def _log_sigmoid(x):
    return jnp.minimum(x, 0.0) - jnp.log1p(jnp.exp(-jnp.abs(x)))


def _dot(a, b):
    return jnp.dot(a, b, preferred_element_type=F32)


def _ffn_math(x, g, wg_ref, wu_ref, wd_ref):
    xn = _rmsnorm(x, g).astype(BF16)
    acc = None
    for c in range(D_FF // FF_COLS):
        cols = slice(c * FF_COLS, (c + 1) * FF_COLS)
        gate = _dot(xn, wg_ref[:, cols])
        up = _dot(xn, wu_ref[:, cols])
        act = (_silu(gate) * up).astype(BF16)
        part = _dot(act, wd_ref[cols, :])
        acc = part if acc is None else acc + part
    return x + 0.5 * acc


def _ffn_body(h_ref, g_ref, wg_ref, wu_ref, wd_ref, *rest):
    o_ref = rest[-1]
    y = _ffn_math(h_ref[...], g_ref[...], wg_ref, wu_ref, wd_ref)
    if len(rest) == 2:
        y = _rmsnorm(y, rest[0][...])
    o_ref[...] = y


def _ffn(h, w, layer, tm, final_gain=None):
    rows = h.shape[0]
    row_spec = pl.BlockSpec((tm, D_MODEL), lambda i: (i, 0))
    in_specs = [row_spec, _layer_spec((1, D_MODEL), layer), _const_spec((D_MODEL, D_FF)),
                _const_spec((D_MODEL, D_FF)), _const_spec((D_FF, D_MODEL))]
    args = [h, *w]
    if final_gain is not None:
        in_specs.append(_const_spec((1, D_MODEL)))
        args.append(final_gain)
    return pl.pallas_call(
        _ffn_body,
        out_shape=jax.ShapeDtypeStruct((rows, D_MODEL), F32),
        grid=(rows // tm,),
        in_specs=in_specs,
        out_specs=row_spec,
        compiler_params=_cparams(1),
        name="ffn",
    )(*args)


def _ffn_cast_body(h_ref, g_ref, wg_hbm, wu_hbm, wd_hbm, *rest, layer, final):
    (o_ref, wg_out, wu_out, wd_out, wg_s, wu_s, wd_s, stage_c, stage_r, in_sem, out_sem) = rest[-11:]
    i = pl.program_id(0)
    n_col = D_FF // FF_COLS
    n_chunks = 3 * n_col

    def chunk_copy(k):
        m, c = divmod(k, n_col)
        blk = pl.ds(c * FF_COLS, FF_COLS)
        slot = k % 2
        if m == 0:
            src, stage = wg_hbm.at[layer, :, blk], stage_c
        elif m == 1:
            src, stage = wu_hbm.at[layer, :, blk], stage_c
        else:
            src, stage = wd_hbm.at[layer, blk, :], stage_r
        return pltpu.make_async_copy(src, stage.at[slot], in_sem.at[slot])

    def write_back(j):
        src, dst = ((wg_s, wg_out), (wu_s, wu_out), (wd_s, wd_out))[j]
        return pltpu.make_async_copy(src, dst, out_sem.at[j])

    @pl.when(i == 0)
    def _():
        chunk_copy(0).start()
        for k in range(n_chunks):
            if k + 1 < n_chunks:
                chunk_copy(k + 1).start()
            chunk_copy(k).wait()
            m, c = divmod(k, n_col)
            blk = slice(c * FF_COLS, (c + 1) * FF_COLS)
            if m == 0:
                wg_s[:, blk] = stage_c[k % 2].astype(BF16)
            elif m == 1:
                wu_s[:, blk] = stage_c[k % 2].astype(BF16)
            else:
                wd_s[blk, :] = stage_r[k % 2].astype(BF16)
        for j in range(3):
            write_back(j).start()

    y = _ffn_math(h_ref[...], g_ref[...], wg_s, wu_s, wd_s)
    if final:
        y = _rmsnorm(y, rest[0][...])
    o_ref[...] = y

    @pl.when(i == pl.num_programs(0) - 1)
    def _():
        for j in range(3):
            write_back(j).wait()


def _ffn_cast(h, g, wg, wu, wd, layer, tm, final_gain=None):
    rows = h.shape[0]
    assert rows // tm >= 2
    row_spec = pl.BlockSpec((tm, D_MODEL), lambda i: (i, 0))
    hbm = pl.BlockSpec(memory_space=pl.ANY)
    in_specs = [row_spec, _layer_spec((1, D_MODEL), layer), hbm, hbm, hbm]
    args = [h, g, wg, wu, wd]
    if final_gain is not None:
        in_specs.append(_const_spec((1, D_MODEL)))
        args.append(final_gain)
    outs = pl.pallas_call(
        functools.partial(_ffn_cast_body, layer=layer, final=final_gain is not None),
        out_shape=[jax.ShapeDtypeStruct((rows, D_MODEL), F32), jax.ShapeDtypeStruct((D_MODEL, D_FF), BF16),
                   jax.ShapeDtypeStruct((D_MODEL, D_FF), BF16), jax.ShapeDtypeStruct((D_FF, D_MODEL), BF16)],
        grid=(rows // tm,),
        in_specs=in_specs,
        out_specs=[row_spec, hbm, hbm, hbm],
        scratch_shapes=[pltpu.VMEM((D_MODEL, D_FF), BF16), pltpu.VMEM((D_MODEL, D_FF), BF16),
                        pltpu.VMEM((D_FF, D_MODEL), BF16), pltpu.VMEM((2, D_MODEL, FF_COLS), F32),
                        pltpu.VMEM((2, FF_COLS, D_MODEL), F32), pltpu.SemaphoreType.DMA((2,)),
                        pltpu.SemaphoreType.DMA((3,))],
        compiler_params=_cparams(1),
        name="ffn_cast",
    )(*args)
    return outs[0], tuple(outs[1:])


_OFF_Q, _OFF_K, _OFF_V, _OFF_G, _OFF_LR, _OFF_A, _OFF_GA, _OFF_MA, _OFF_MB, D_IN = (
    0, 512, 1024, 2048, 3072, 3088, 4112, 5136, 6160, 7184)


def _proj_body(h_ref, g_ref, wt_ref, wdu_ref, bdec_ref,
               q_ref, k_ref, la_ref, v_ref, gs_ref, u_ref, sa_ref, sb_ref):
    xn = _rmsnorm(h_ref[...], g_ref[...]).astype(BF16)

    def mm(lo, hi):
        return lax.dot_general(xn, wt_ref[lo:hi, :], NT_DIMS, preferred_element_type=F32)

    q_ref[...] = mm(_OFF_Q, _OFF_K) * (GLA_DK ** -0.5)
    k_ref[...] = mm(_OFF_K, _OFF_V)
    v_ref[...] = mm(_OFF_V, _OFF_G).astype(BF16)
    gs_ref[...] = _silu(mm(_OFF_G, _OFF_LR)).astype(BF16)
    u_ref[...] = mm(_OFF_A, _OFF_GA) * _sigmoid(mm(_OFF_GA, _OFF_MA))
    sa_ref[...] = _sigmoid(mm(_OFF_MA, _OFF_MB)).astype(BF16)
    sb_ref[...] = _sigmoid(mm(_OFF_MB, D_IN)).astype(BF16)
    lane = lax.broadcasted_iota(jnp.int32, (xn.shape[0], LANE), 1)
    a_lr = jnp.where(lane < GATE_RANK, mm(_OFF_LR, _OFF_LR + LANE), 0.0).astype(BF16)
    z = _dot(a_lr, wdu_ref[...]) + bdec_ref[...]
    la_ref[...] = _log_sigmoid(z) / GATE_TAU


def _proj(h, w, layer, tm):
    rows = h.shape[0]

    def rs(n):
        return pl.BlockSpec((tm, n), lambda i: (i, 0))

    outs = ((GLA_KEY, F32), (GLA_KEY, F32), (GLA_KEY, F32), (GLA_VAL, BF16), (GLA_VAL, BF16),
            (D_MODEL, F32), (D_MODEL, BF16), (D_MODEL, BF16))
    return pl.pallas_call(
        _proj_body,
        out_shape=[jax.ShapeDtypeStruct((rows, n), dt) for n, dt in outs],
        grid=(rows // tm,),
        in_specs=[rs(D_MODEL), _layer_spec((1, D_MODEL), layer), _layer_spec((D_IN, D_MODEL), layer),
                  _layer_spec((LANE, GLA_KEY), layer), _layer_spec((1, GLA_KEY), layer)],
        out_specs=[rs(n) for n, _ in outs],
        compiler_params=_cparams(1),
        name="proj",
    )(h, *w)


def _chunk_cumsum(x, chunk):
    pos = lax.broadcasted_iota(jnp.int32, x.shape, 0) % chunk
    shift = 1
    while shift < chunk:
        x = x + jnp.where(pos >= shift, pltpu.roll(x, shift, 0), 0.0)
        shift *= 2
    return x


def _head_norm_gate(o, gnorm, gs):
    return o * lax.rsqrt(jnp.mean(o * o, axis=-1, keepdims=True) + EPS) * gnorm * gs


def _gla_blocks(q, k, v, la, gs, gnorm, states, blk, store):
    n_blk = q.shape[0] // blk
    heads = range(GLA_HEADS)
    kcs = [slice(h * GLA_DK, (h + 1) * GLA_DK) for h in heads]
    vcs = [slice(h * GLA_DV, (h + 1) * GLA_DV) for h in heads]
    carried = not isinstance(states[0], list)
    b_all = _chunk_cumsum(la, blk)
    q_t, k_t, k_u, v_b, dec = [], [], [], [], []
    for j in range(n_blk):
        rows = slice(j * blk, (j + 1) * blk)
        b = b_all[rows, :]
        b_last = b[blk - 1:blk, :]
        kj = k[rows, :]
        q_t.append((q[rows, :] * jnp.exp(b)).astype(BF16))
        k_t.append((kj * jnp.exp(-b)).astype(BF16))
        k_u.append((kj * jnp.exp(b_last - b)).astype(BF16))
        v_b.append(v[rows, :])
        dec.append(jnp.exp(b_last))
    tri = lax.broadcasted_iota(jnp.int32, (blk, blk), 0) >= lax.broadcasted_iota(jnp.int32, (blk, blk), 1)
    eye = (lax.broadcasted_iota(jnp.int32, (GLA_DK, GLA_DK), 0)
           == lax.broadcasted_iota(jnp.int32, (GLA_DK, GLA_DK), 1))
    probs = [[jnp.where(tri, lax.dot_general(q_t[j][:, kcs[h]], k_t[j][:, kcs[h]], NT_DIMS,
                                            preferred_element_type=F32), 0.0).astype(BF16)
              for h in heads] for j in range(n_blk)]
    upd = [[lax.dot_general(k_u[j][:, kcs[h]], v_b[j][:, vcs[h]], TN_DIMS, preferred_element_type=F32)
            for h in heads] for j in range(n_blk)]
    dec_col = [[jnp.sum(jnp.where(eye, jnp.broadcast_to(dec[j][:, kcs[h]], (GLA_DK, GLA_DK)), 0.0),
                        axis=1, keepdims=True) for h in heads] for j in range(n_blk)]
    finals = []
    cur = list(states) if carried else None
    for j in range(n_blk):
        rows = slice(j * blk, (j + 1) * blk)
        if not carried:
            cur = list(states[j])
        for h in heads:
            o = _dot(probs[j][h], v_b[j][:, vcs[h]]) + _dot(q_t[j][:, kcs[h]], cur[h].astype(BF16))
            cur[h] = dec_col[j][h] * cur[h] + upd[j][h]
            store(j, h, _head_norm_gate(o, gnorm[:, vcs[h]], gs[rows, vcs[h]]))
        finals.append(list(cur))
    return finals


def _gla_step_body(q_ref, k_ref, la_ref, v_ref, gs_ref, s0_ref, gn_ref, *rest, sb, seq):
    o_ref, sfin_ref = rest[-2:]

    def store(j, h, o):
        o_ref[j * seq:(j + 1) * seq, h * GLA_DV:(h + 1) * GLA_DV] = o.astype(o_ref.dtype)

    states = [[s0_ref[i, h] for h in range(GLA_HEADS)] for i in range(sb)]
    finals = _gla_blocks(q_ref[...], k_ref[...], v_ref[...], la_ref[...], gs_ref[...].astype(F32), gn_ref[...],
                         states, seq, store)
    for i in range(sb):
        for h in range(GLA_HEADS):
            sfin_ref[i, h] = finals[i][h]


def _gla_step(q, k, la, v, gs, s0_all, gnorm, layer, state_layer, seq, sb, row0, n_seq, prev_state):
    blk = sb * seq
    base = row0 // blk

    def bs(n, first=base):
        return pl.BlockSpec((blk, n), lambda i: (first + i, 0))

    st = pl.BlockSpec((None, sb) + STATE_TAIL, lambda i: (state_layer, i, 0, 0, 0))
    in_specs = [bs(GLA_KEY), bs(GLA_KEY), bs(GLA_KEY), bs(GLA_VAL), bs(GLA_VAL), st, _layer_spec((1, GLA_VAL), layer)]
    args = [q, k, la, v, gs, s0_all, gnorm]
    aliases = _alias_prev(in_specs, args, prev_state, 1)
    return pl.pallas_call(
        functools.partial(_gla_step_body, sb=sb, seq=seq),
        out_shape=[jax.ShapeDtypeStruct((n_seq * seq, GLA_VAL), F32), jax.ShapeDtypeStruct(s0_all.shape, F32)],
        grid=(n_seq // sb,),
        in_specs=in_specs,
        out_specs=[bs(GLA_VAL, 0), st],
        input_output_aliases=aliases,
        compiler_params=_cparams(1),
        name="gla_step",
    )(*args)


def _conv_act(acc, cn):
    return _silu(_rmsnorm(acc, cn))


def _conv_step_body(u_ref, cache_ref, wdw_ref, bdw_ref, cn_ref, *rest, sb, seq):
    c_ref, ncache_ref = rest[-2:]
    u_t = jnp.swapaxes(u_ref[...].reshape(sb, seq, D_MODEL), 0, 1)

    def up(m):
        return cache_ref[m] if m < CONV_HIST else u_t[m - CONV_HIST]

    outs = []
    for t in range(seq):
        acc = jnp.broadcast_to(bdw_ref[...], (sb, D_MODEL))
        for j in range(CONV_WIDTH):
            acc = acc + wdw_ref[j:j + 1, :] * up(t + j)
        outs.append(_conv_act(acc, cn_ref[...]))
    c_ref[...] = jnp.swapaxes(jnp.stack(outs), 0, 1).reshape(sb * seq, D_MODEL)
    for m in range(CONV_HIST):
        ncache_ref[m] = up(m + seq)


def _conv_step(u, cache_all, conv_w, layer, cache_layer, seq, sb, row0, n_seq, prev_cache):
    blk = sb * seq
    base = row0 // blk
    us = pl.BlockSpec((blk, D_MODEL), lambda i: (base + i, 0))
    cs = pl.BlockSpec((None, CONV_HIST, sb, D_MODEL), lambda i: (cache_layer, 0, i, 0))
    in_specs = [us, cs, _layer_spec((CONV_WIDTH, D_MODEL), layer), _layer_spec((1, D_MODEL), layer),
                _layer_spec((1, D_MODEL), layer)]
    args = [u, cache_all, *conv_w]
    aliases = _alias_prev(in_specs, args, prev_cache, 1)
    return pl.pallas_call(
        functools.partial(_conv_step_body, sb=sb, seq=seq),
        out_shape=[jax.ShapeDtypeStruct((n_seq * seq, D_MODEL), F32), jax.ShapeDtypeStruct(cache_all.shape, F32)],
        grid=(n_seq // sb,),
        in_specs=in_specs,
        out_specs=[pl.BlockSpec((blk, D_MODEL), lambda i: (i, 0)), cs],
        input_output_aliases=aliases,
        compiler_params=_cparams(1),
        name="conv_step",
    )(*args)


def _conv_piece(ubuf, wdw_ref, r0, l0):
    pad = HIST_ROWS - CONV_HIST
    xs = ubuf[r0:r0 + CONV_RB + HIST_ROWS, l0:l0 + LANE]
    out = None
    for s in range(8):
        rows = CONV_RB if s == 0 else CONV_RB + 8
        z = None
        for a in range(HIST_ROWS // 8 + 1):
            j = 8 * a + s - pad
            if 0 <= j < CONV_WIDTH:
                term = wdw_ref[j:j + 1, l0:l0 + LANE] * xs[8 * a:8 * a + rows]
                z = term if z is None else z + term
        z = z[s:s + CONV_RB]
        out = z if out is None else out + z
    return out


def _merge_math(h, c_act, og, sa, sb, wpw_ref, wout_ref):
    o_conv = _dot(c_act, wpw_ref[...])
    merged = (sa * og + sb * o_conv).astype(BF16)
    return h + _dot(merged, wout_ref[...])


def _merge_body(h_ref, ca_ref, cb_ref, oga_ref, ogb_ref, sa_ref, sb_ref, wpw_ref, wout_ref, o_ref):
    c_act = jnp.concatenate([ca_ref[...], cb_ref[...]], axis=0).astype(BF16)
    og = jnp.concatenate([oga_ref[...], ogb_ref[...]], axis=0)
    o_ref[...] = _merge_math(h_ref[...], c_act, og, sa_ref[...].astype(F32), sb_ref[...].astype(F32),
                             wpw_ref, wout_ref)


def _merge(h, c_parts, og_parts, sa, sb, merge_w, layer):
    rows = h.shape[0]

    def whole(a):
        return _const_spec(a.shape)

    sq = _layer_spec((D_MODEL, D_MODEL), layer)
    args = [h, *c_parts, *og_parts, sa, sb]
    return pl.pallas_call(
        _merge_body,
        out_shape=jax.ShapeDtypeStruct((rows, D_MODEL), F32),
        grid=(1,),
        in_specs=[whole(a) for a in args] + [sq, sq],
        out_specs=pl.BlockSpec((rows, D_MODEL), lambda i: (0, 0)),
        compiler_params=_cparams(1),
        name="merge",
    )(*args, *merge_w)


def _mixer_seq_body(q_ref, k_ref, la_ref, v_ref, gs_ref, s0_ref, gn_ref, u_ref, cache_ref, h_ref, sa_ref, sb_ref,
                    wdw_ref, bdw_ref, cn_ref, wpw_ref, wout_ref, o_ref, sfin_ref, ncache_ref,
                    s_scr, og_scr, ubuf, cbuf, *, tl, chunk):
    t = pl.program_id(1)
    pad = HIST_ROWS - CONV_HIST

    @pl.when(t == 0)
    def _():
        s_scr[...] = s0_ref[...]
        ubuf[0:pad, :] = jnp.zeros((pad, D_MODEL), F32)
        ubuf[pad:HIST_ROWS, :] = cache_ref[...]

    @pl.when(t > 0)
    def _():
        ubuf[0:HIST_ROWS, :] = ubuf[tl:tl + HIST_ROWS, :]

    def store(j, h, o):
        og_scr[j * chunk:(j + 1) * chunk, h * GLA_DV:(h + 1) * GLA_DV] = o

    finals = _gla_blocks(q_ref[...], k_ref[...], v_ref[...], la_ref[...], gs_ref[...].astype(F32), gn_ref[...],
                         [s_scr[h] for h in range(GLA_HEADS)], chunk, store)
    for h in range(GLA_HEADS):
        s_scr[h] = finals[-1][h]

    ubuf[HIST_ROWS:HIST_ROWS + tl, :] = u_ref[...]
    for rb in range(tl // CONV_RB):
        rows = slice(rb * CONV_RB, (rb + 1) * CONV_RB)
        for lc in range(D_MODEL // LANE):
            lanes = slice(lc * LANE, (lc + 1) * LANE)
            cbuf[rows, lanes] = _conv_piece(ubuf, wdw_ref, rb * CONV_RB, lc * LANE) + bdw_ref[:, lanes]
    c_act = _conv_act(cbuf[...], cn_ref[...]).astype(BF16)
    o_ref[...] = _merge_math(h_ref[...], c_act, og_scr[...], sa_ref[...].astype(F32), sb_ref[...].astype(F32),
                             wpw_ref, wout_ref)

    @pl.when(t == pl.num_programs(1) - 1)
    def _():
        sfin_ref[...] = s_scr[...]
        ncache_ref[...] = ubuf[tl + pad:tl + HIST_ROWS, :]


def _mixer_seq(q, k, la, v, gs, u, h, sa, sb, s0, cache, gnorm, conv_w, merge_w, layer, tl, chunk):
    bsz, seq, _ = h.shape

    def ts(n):
        return pl.BlockSpec((None, tl, n), lambda b, t: (b, t, 0))

    vec = _layer_spec((1, D_MODEL), layer)
    sq = _layer_spec((D_MODEL, D_MODEL), layer)
    return pl.pallas_call(
        functools.partial(_mixer_seq_body, tl=tl, chunk=chunk),
        out_shape=[jax.ShapeDtypeStruct((bsz, seq, D_MODEL), F32),
                   jax.ShapeDtypeStruct((bsz,) + STATE_TAIL, F32),
                   jax.ShapeDtypeStruct((bsz, CONV_HIST, D_MODEL), F32)],
        grid=(bsz, seq // tl),
        in_specs=[ts(GLA_KEY), ts(GLA_KEY), ts(GLA_KEY), ts(GLA_VAL), ts(GLA_VAL),
                  _const_spec(STATE_TAIL), _layer_spec((1, GLA_VAL), layer),
                  ts(D_MODEL), _const_spec((CONV_HIST, D_MODEL)), ts(D_MODEL), ts(D_MODEL), ts(D_MODEL),
                  _layer_spec((CONV_WIDTH, D_MODEL), layer), vec, vec, sq, sq],
        out_specs=[ts(D_MODEL), pl.BlockSpec((None,) + STATE_TAIL, lambda b, t: (b, 0, 0, 0)),
                   pl.BlockSpec((None, CONV_HIST, D_MODEL), lambda b, t: (b, 0, 0))],
        scratch_shapes=[pltpu.VMEM(STATE_TAIL, F32), pltpu.VMEM((tl, GLA_VAL), F32),
                        pltpu.VMEM((tl + HIST_ROWS, D_MODEL), F32), pltpu.VMEM((tl, D_MODEL), F32)],
        compiler_params=_cparams(2),
        name="mixer_seq",
    )(q, k, la, v, gs, s0, gnorm, u, cache, h, sa, sb, *conv_w, *merge_w)


def kernel(x_prompt, x_sample, state_gla, cache_conv, meta_tokens, norm_ffn1, w_ffn1_gate, w_ffn1_up, w_ffn1_down, norm_mix, w_in, w_decay_up, b_decay, gla_norm, w_dw, b_dw, conv_norm, w_pw, w_out, norm_ffn2, w_ffn2_gate, w_ffn2_up, w_ffn2_down, norm_final):
    n_prompt, l_prompt, _ = x_prompt.shape
    n_sample, l_sample, _ = x_sample.shape
    rows_p = n_prompt * l_prompt
    rows_s = n_sample * l_sample
    rows_small = rows_s + N_META

    hp = x_prompt.reshape(rows_p, D_MODEL)
    hs = jnp.concatenate([x_sample.reshape(rows_s, D_MODEL), meta_tokens], axis=0)

    tm_ffn_p, tm_proj_p, tl_prompt = 512, 512, 512
    tm_proj_s = 208
    sb_gla, sb_conv = 8, 32

    def vec(a):
        return a.reshape(a.shape[0], 1, a.shape[-1])

    def bf(a):
        return a.astype(BF16)

    w_in_t = bf(jnp.swapaxes(w_in, 1, 2))
    w_du = jnp.pad(w_decay_up, ((0, 0), (0, LANE - GATE_RANK), (0, 0))).astype(BF16)

    ffn1_p = (vec(norm_ffn1), w_ffn1_gate, w_ffn1_up, w_ffn1_down)
    ffn2_p = (vec(norm_ffn2), w_ffn2_gate, w_ffn2_up, w_ffn2_down)
    proj_w = (vec(norm_mix), w_in_t, w_du, vec(b_decay))
    conv_w = (w_dw, vec(b_dw), vec(conv_norm))
    merge_w = (bf(w_pw), bf(w_out))
    gnorm = vec(gla_norm)
    gfin = norm_final.reshape(1, D_MODEL)

    zero_state = jnp.zeros((1, 1) + STATE_TAIL, F32)
    zero_cache = jnp.zeros((1, CONV_HIST, 1, D_MODEL), F32)
    cache_tm = jnp.swapaxes(cache_conv, 1, 2)

    def pseq(a):
        return a.reshape(n_prompt, l_prompt, a.shape[-1])

    sp_list, cp_list = [], []
    state_s = cache_s = None
    for l in range(DEPTH):
        last = l == DEPTH - 1
        hp, w1 = _ffn_cast(hp, *ffn1_p, l, tm_ffn_p)
        hs = _ffn(hs, (ffn1_p[0], *w1), l, rows_small)

        q, k, la, v, gs, u, sa, sb = _proj(hs, proj_w, l, tm_proj_s)
        og_s, state_s = _gla_step(q, k, la, v, gs, state_gla, gnorm, l, l, l_sample, sb_gla, 0, n_sample, state_s)
        og_m, s_m = _gla_step(q, k, la, v, gs, zero_state, gnorm, l, 0, N_META, 1, rows_s, 1, None)
        c_s, cache_s = _conv_step(u, cache_tm, conv_w, l, l, l_sample, sb_conv, 0, n_sample, cache_s)
        c_m, nc_m = _conv_step(u, zero_cache, conv_w, l, 0, N_META, 1, rows_s, 1, None)
        hs = _merge(hs, (c_s, c_m), (og_s, og_m), sa, sb, merge_w, l)

        q, k, la, v, gs, u, sa, sb = _proj(hp, proj_w, l, tm_proj_p)
        hp, s_p, nc_p = _mixer_seq(pseq(q), pseq(k), pseq(la), pseq(v), pseq(gs), pseq(u), pseq(hp), pseq(sa),
                                   pseq(sb), s_m[0, 0], nc_m[0, :, 0, :], gnorm, conv_w, merge_w, l, tl_prompt, CHUNK)
        hp, w2 = _ffn_cast(hp.reshape(rows_p, D_MODEL), *ffn2_p, l, tm_ffn_p, gfin if last else None)
        hs = _ffn(hs, (ffn2_p[0], *w2), l, rows_small, gfin if last else None)
        sp_list.append(s_p)
        cp_list.append(nc_p)

    y_prompt = hp.reshape(n_prompt, l_prompt, D_MODEL)
    y_sample = hs[:rows_s].reshape(n_sample, l_sample, D_MODEL)
    return (y_prompt, y_sample, jnp.stack(sp_list), jnp.stack(cp_list), state_s, jnp.swapaxes(cache_s, 1, 2))
```

```python
import functools

import jax
import jax.numpy as jnp
from jax import lax
from jax.experimental import pallas as pl
from jax.experimental.pallas import tpu as pltpu

F32 = jnp.float32
BF16 = jnp.bfloat16

D_MODEL = 1024
DEPTH = 4
N_META = 16
GLA_HEADS = 4
GLA_DK = 128
GLA_DV = 256
GLA_KEY = GLA_HEADS * GLA_DK
GLA_VAL = GLA_HEADS * GLA_DV
GATE_RANK = 16
GATE_TAU = 16.0
CHUNK = 64
CONV_WIDTH = 31
CONV_HIST = CONV_WIDTH - 1
D_FF = 2816
EPS = 1e-6

LANE = 128
FF_COLS = 256
HIST_ROWS = 32
CONV_RB = 128
STAGE_SLOTS = 3
STAGE_ROWS_WIDE = 128
VMEM_LIMIT = 56 * 1024 * 1024

STATE_TAIL = (GLA_HEADS, GLA_DK, GLA_DV)
NT_DIMS = (((1,), (1,)), ((), ()))
TN_DIMS = (((0,), (0,)), ((), ()))


def _cparams(n_axes):
    return pltpu.CompilerParams(dimension_semantics=("arbitrary",) * n_axes, vmem_limit_bytes=VMEM_LIMIT)


def _const_spec(shape):
    nd = len(shape)
    return pl.BlockSpec(shape, lambda *_: (0,) * nd, pipeline_mode=pl.Buffered(1))


def _layer_spec(shape, layer):
    nd = len(shape)
    return pl.BlockSpec((None,) + tuple(shape), lambda *_: (layer,) + (0,) * nd, pipeline_mode=pl.Buffered(1))


def _alias_prev(in_specs, args, prev_out, out_index):
    if prev_out is None:
        return {}
    in_specs.append(pl.BlockSpec(memory_space=pl.ANY))
    args.append(prev_out)
    return {len(args) - 1: out_index}


def _rmsnorm(x, g):
    return x * lax.rsqrt(jnp.mean(x * x, axis=-1, keepdims=True) + EPS) * g


def _sigmoid(x):
    return 1.0 / (1.0 + jnp.exp(-x))


def _silu(x):
    return x * _sigmoid(x)


def _log_sigmoid(x):
    return jnp.minimum(x, 0.0) - jnp.log1p(jnp.exp(-jnp.abs(x)))


def _dot(a, b):
    return jnp.dot(a, b, preferred_element_type=F32)


def _ffn_math(x, g, wg_ref, wu_ref, wd_ref):
    xn = _rmsnorm(x, g).astype(BF16)
    acc = None
    for c in range(D_FF // FF_COLS):
        cols = slice(c * FF_COLS, (c + 1) * FF_COLS)
        gate = _dot(xn, wg_ref[:, cols])
        up = _dot(xn, wu_ref[:, cols])
        act = (_silu(gate) * up).astype(BF16)
        part = _dot(act, wd_ref[cols, :])
        acc = part if acc is None else acc + part
    return x + 0.5 * acc


def _ffn_body(h_ref, g_ref, wg_ref, wu_ref, wd_ref, *rest):
    o_ref = rest[-1]
    y = _ffn_math(h_ref[...], g_ref[...], wg_ref, wu_ref, wd_ref)
    if len(rest) == 2:
        y = _rmsnorm(y, rest[0][...])
    o_ref[...] = y


def _ffn(h, w, layer, tm, final_gain=None):
    rows = h.shape[0]
    row_spec = pl.BlockSpec((tm, D_MODEL), lambda i: (i, 0))
    in_specs = [row_spec, _layer_spec((1, D_MODEL), layer), _const_spec((D_MODEL, D_FF)),
                _const_spec((D_MODEL, D_FF)), _const_spec((D_FF, D_MODEL))]
    args = [h, *w]
    if final_gain is not None:
        in_specs.append(_const_spec((1, D_MODEL)))
        args.append(final_gain)
    return pl.pallas_call(
        _ffn_body,
        out_shape=jax.ShapeDtypeStruct((rows, D_MODEL), F32),
        grid=(rows // tm,),
        in_specs=in_specs,
        out_specs=row_spec,
        compiler_params=_cparams(1),
        name="ffn",
    )(*args)


def _ffn_cast_body(h_ref, g_ref, wg_hbm, wu_hbm, wd_hbm, *rest, layer, final):
    (o_ref, wg_out, wu_out, wd_out, wg_s, wu_s, wd_s, stage_a, stage_b, in_sem, out_sem) = rest[-11:]
    i = pl.program_id(0)
    rows_a, rows_b = stage_a.shape[1], stage_b.shape[1]
    n_a, n_b = D_MODEL // rows_a, D_FF // rows_b
    n_chunks = 2 * n_a + n_b

    def chunk(k):
        if k < 2 * n_a:
            hbm, dst = (wg_hbm, wg_s) if k < n_a else (wu_hbm, wu_s)
            return hbm, stage_a, dst, pl.ds((k % n_a) * rows_a, rows_a)
        return wd_hbm, stage_b, wd_s, pl.ds((k - 2 * n_a) * rows_b, rows_b)

    def chunk_copy(k):
        hbm, stage, _, rows = chunk(k)
        slot = k % STAGE_SLOTS
        return pltpu.make_async_copy(hbm.at[layer, rows, :], stage.at[slot], in_sem.at[slot])

    def write_back(j):
        src, dst = ((wg_s, wg_out), (wu_s, wu_out), (wd_s, wd_out))[j]
        return pltpu.make_async_copy(src, dst, out_sem.at[j])

    @pl.when(i == 0)
    def _():
        ahead = STAGE_SLOTS - 1
        for k in range(min(ahead, n_chunks)):
            chunk_copy(k).start()
        for k in range(n_chunks):
            if k + ahead < n_chunks:
                chunk_copy(k + ahead).start()
            chunk_copy(k).wait()
            _, stage, dst, rows = chunk(k)
            dst[rows, :] = stage[k % STAGE_SLOTS].astype(BF16)
        for j in range(3):
            write_back(j).start()

    y = _ffn_math(h_ref[...], g_ref[...], wg_s, wu_s, wd_s)
    if final:
        y = _rmsnorm(y, rest[0][...])
    o_ref[...] = y

    @pl.when(i == pl.num_programs(0) - 1)
    def _():
        for j in range(3):
            write_back(j).wait()


def _ffn_cast(h, g, wg, wu, wd, layer, tm, final_gain=None):
    rows = h.shape[0]
    assert rows // tm >= 2
    row_spec = pl.BlockSpec((tm, D_MODEL), lambda i: (i, 0))
    hbm = pl.BlockSpec(memory_space=pl.ANY)
    in_specs = [row_spec, _layer_spec((1, D_MODEL), layer), hbm, hbm, hbm]
    args = [h, g, wg, wu, wd]
    if final_gain is not None:
        in_specs.append(_const_spec((1, D_MODEL)))
        args.append(final_gain)
    outs = pl.pallas_call(
        functools.partial(_ffn_cast_body, layer=layer, final=final_gain is not None),
        out_shape=[jax.ShapeDtypeStruct((rows, D_MODEL), F32), jax.ShapeDtypeStruct((D_MODEL, D_FF), BF16),
                   jax.ShapeDtypeStruct((D_MODEL, D_FF), BF16), jax.ShapeDtypeStruct((D_FF, D_MODEL), BF16)],
        grid=(rows // tm,),
        in_specs=in_specs,
        out_specs=[row_spec, hbm, hbm, hbm],
        scratch_shapes=[pltpu.VMEM((D_MODEL, D_FF), BF16), pltpu.VMEM((D_MODEL, D_FF), BF16),
                        pltpu.VMEM((D_FF, D_MODEL), BF16),
                        pltpu.VMEM((STAGE_SLOTS, STAGE_ROWS_WIDE, D_FF), F32),
                        pltpu.VMEM((STAGE_SLOTS, FF_COLS, D_MODEL), F32),
                        pltpu.SemaphoreType.DMA((STAGE_SLOTS,)), pltpu.SemaphoreType.DMA((3,))],
        compiler_params=_cparams(1),
        name="ffn_cast",
    )(*args)
    return outs[0], tuple(outs[1:])


_OFF_Q, _OFF_K, _OFF_V, _OFF_G, _OFF_LR, _OFF_A, _OFF_GA, _OFF_MA, _OFF_MB, D_IN = (
    0, 512, 1024, 2048, 3072, 3088, 4112, 5136, 6160, 7184)


def _proj_body(h_ref, g_ref, wt_ref, wdu_ref, bdec_ref,
               q_ref, k_ref, la_ref, v_ref, gs_ref, u_ref, sa_ref, sb_ref):
    xn = _rmsnorm(h_ref[...], g_ref[...]).astype(BF16)

    def mm(lo, hi):
        return lax.dot_general(xn, wt_ref[lo:hi, :], NT_DIMS, preferred_element_type=F32)

    q_ref[...] = mm(_OFF_Q, _OFF_K) * (GLA_DK ** -0.5)
    k_ref[...] = mm(_OFF_K, _OFF_V)
    v_ref[...] = mm(_OFF_V, _OFF_G).astype(BF16)
    gs_ref[...] = _silu(mm(_OFF_G, _OFF_LR)).astype(BF16)
    u_ref[...] = mm(_OFF_A, _OFF_GA) * _sigmoid(mm(_OFF_GA, _OFF_MA))
    sa_ref[...] = _sigmoid(mm(_OFF_MA, _OFF_MB)).astype(BF16)
    sb_ref[...] = _sigmoid(mm(_OFF_MB, D_IN)).astype(BF16)
    lane = lax.broadcasted_iota(jnp.int32, (xn.shape[0], LANE), 1)
    a_lr = jnp.where(lane < GATE_RANK, mm(_OFF_LR, _OFF_LR + LANE), 0.0).astype(BF16)
    z = _dot(a_lr, wdu_ref[...]) + bdec_ref[...]
    la_ref[...] = _log_sigmoid(z) / GATE_TAU


def _proj(h, w, layer, tm):
    rows = h.shape[0]

    def rs(n):
        return pl.BlockSpec((tm, n), lambda i: (i, 0))

    outs = ((GLA_KEY, F32), (GLA_KEY, F32), (GLA_KEY, F32), (GLA_VAL, BF16), (GLA_VAL, BF16),
            (D_MODEL, F32), (D_MODEL, BF16), (D_MODEL, BF16))
    return pl.pallas_call(
        _proj_body,
        out_shape=[jax.ShapeDtypeStruct((rows, n), dt) for n, dt in outs],
        grid=(rows // tm,),
        in_specs=[rs(D_MODEL), _layer_spec((1, D_MODEL), layer), _layer_spec((D_IN, D_MODEL), layer),
                  _layer_spec((LANE, GLA_KEY), layer), _layer_spec((1, GLA_KEY), layer)],
        out_specs=[rs(n) for n, _ in outs],
        compiler_params=_cparams(1),
        name="proj",
    )(h, *w)


def _chunk_cumsum(x, chunk):
    pos = lax.broadcasted_iota(jnp.int32, x.shape, 0) % chunk
    shift = 1
    while shift < chunk:
        x = x + jnp.where(pos >= shift, pltpu.roll(x, shift, 0), 0.0)
        shift *= 2
    return x


def _head_norm_gate(o, gnorm, gs):
    return o * lax.rsqrt(jnp.mean(o * o, axis=-1, keepdims=True) + EPS) * gnorm * gs


def _gla_blocks(q, k, v, la, gs, gnorm, states, blk, store):
    n_blk = q.shape[0] // blk
    heads = range(GLA_HEADS)
    kcs = [slice(h * GLA_DK, (h + 1) * GLA_DK) for h in heads]
    vcs = [slice(h * GLA_DV, (h + 1) * GLA_DV) for h in heads]
    carried = not isinstance(states[0], list)
    b_all = _chunk_cumsum(la, blk)
    q_t, k_t, k_u, v_b, dec = [], [], [], [], []
    for j in range(n_blk):
        rows = slice(j * blk, (j + 1) * blk)
        b = b_all[rows, :]
        b_last = b[blk - 1:blk, :]
        kj = k[rows, :]
        q_t.append((q[rows, :] * jnp.exp(b)).astype(BF16))
        k_t.append((kj * jnp.exp(-b)).astype(BF16))
        k_u.append((kj * jnp.exp(b_last - b)).astype(BF16))
        v_b.append(v[rows, :])
        dec.append(jnp.exp(b_last))
    tri = lax.broadcasted_iota(jnp.int32, (blk, blk), 0) >= lax.broadcasted_iota(jnp.int32, (blk, blk), 1)
    eye = (lax.broadcasted_iota(jnp.int32, (GLA_DK, GLA_DK), 0)
           == lax.broadcasted_iota(jnp.int32, (GLA_DK, GLA_DK), 1))
    probs = [[jnp.where(tri, lax.dot_general(q_t[j][:, kcs[h]], k_t[j][:, kcs[h]], NT_DIMS,
                                            preferred_element_type=F32), 0.0).astype(BF16)
              for h in heads] for j in range(n_blk)]
    upd = [[lax.dot_general(k_u[j][:, kcs[h]], v_b[j][:, vcs[h]], TN_DIMS, preferred_element_type=F32)
            for h in heads] for j in range(n_blk)]
    dec_col = [[jnp.sum(jnp.where(eye, jnp.broadcast_to(dec[j][:, kcs[h]], (GLA_DK, GLA_DK)), 0.0),
                        axis=1, keepdims=True) for h in heads] for j in range(n_blk)]
    finals = []
    cur = list(states) if carried else None
    for j in range(n_blk):
        rows = slice(j * blk, (j + 1) * blk)
        if not carried:
            cur = list(states[j])
        for h in heads:
            o = _dot(probs[j][h], v_b[j][:, vcs[h]]) + _dot(q_t[j][:, kcs[h]], cur[h].astype(BF16))
            cur[h] = dec_col[j][h] * cur[h] + upd[j][h]
            store(j, h, _head_norm_gate(o, gnorm[:, vcs[h]], gs[rows, vcs[h]]))
        finals.append(list(cur))
    return finals


def _gla_step_body(q_ref, k_ref, la_ref, v_ref, gs_ref, s0_ref, gn_ref, *rest, sb, seq):
    o_ref, sfin_ref = rest[-2:]

    def store(j, h, o):
        o_ref[j * seq:(j + 1) * seq, h * GLA_DV:(h + 1) * GLA_DV] = o.astype(o_ref.dtype)

    states = [[s0_ref[i, h] for h in range(GLA_HEADS)] for i in range(sb)]
    finals = _gla_blocks(q_ref[...], k_ref[...], v_ref[...], la_ref[...], gs_ref[...].astype(F32), gn_ref[...],
                         states, seq, store)
    for i in range(sb):
        for h in range(GLA_HEADS):
            sfin_ref[i, h] = finals[i][h]


def _gla_step(q, k, la, v, gs, s0_all, gnorm, layer, state_layer, seq, sb, row0, n_seq, prev_state):
    blk = sb * seq
    base = row0 // blk

    def bs(n, first=base):
        return pl.BlockSpec((blk, n), lambda i: (first + i, 0))

    st = pl.BlockSpec((None, sb) + STATE_TAIL, lambda i: (state_layer, i, 0, 0, 0))
    in_specs = [bs(GLA_KEY), bs(GLA_KEY), bs(GLA_KEY), bs(GLA_VAL), bs(GLA_VAL), st, _layer_spec((1, GLA_VAL), layer)]
    args = [q, k, la, v, gs, s0_all, gnorm]
    aliases = _alias_prev(in_specs, args, prev_state, 1)
    return pl.pallas_call(
        functools.partial(_gla_step_body, sb=sb, seq=seq),
        out_shape=[jax.ShapeDtypeStruct((n_seq * seq, GLA_VAL), F32), jax.ShapeDtypeStruct(s0_all.shape, F32)],
        grid=(n_seq // sb,),
        in_specs=in_specs,
        out_specs=[bs(GLA_VAL, 0), st],
        input_output_aliases=aliases,
        compiler_params=_cparams(1),
        name="gla_step",
    )(*args)


def _conv_act(acc, cn):
    return _silu(_rmsnorm(acc, cn))


def _conv_step_body(u_ref, cache_ref, wdw_ref, bdw_ref, cn_ref, *rest, sb, seq):
    c_ref, ncache_ref = rest[-2:]
    u_t = jnp.swapaxes(u_ref[...].reshape(sb, seq, D_MODEL), 0, 1)

    def up(m):
        return cache_ref[m] if m < CONV_HIST else u_t[m - CONV_HIST]

    outs = []
    for t in range(seq):
        acc = jnp.broadcast_to(bdw_ref[...], (sb, D_MODEL))
        for j in range(CONV_WIDTH):
            acc = acc + wdw_ref[j:j + 1, :] * up(t + j)
        outs.append(_conv_act(acc, cn_ref[...]))
    c_ref[...] = jnp.swapaxes(jnp.stack(outs), 0, 1).reshape(sb * seq, D_MODEL)
    for m in range(CONV_HIST):
        ncache_ref[m] = up(m + seq)


def _conv_step(u, cache_all, conv_w, layer, cache_layer, seq, sb, row0, n_seq, prev_cache):
    blk = sb * seq
    base = row0 // blk
    us = pl.BlockSpec((blk, D_MODEL), lambda i: (base + i, 0))
    cs = pl.BlockSpec((None, CONV_HIST, sb, D_MODEL), lambda i: (cache_layer, 0, i, 0))
    in_specs = [us, cs, _layer_spec((CONV_WIDTH, D_MODEL), layer), _layer_spec((1, D_MODEL), layer),
                _layer_spec((1, D_MODEL), layer)]
    args = [u, cache_all, *conv_w]
    aliases = _alias_prev(in_specs, args, prev_cache, 1)
    return pl.pallas_call(
        functools.partial(_conv_step_body, sb=sb, seq=seq),
        out_shape=[jax.ShapeDtypeStruct((n_seq * seq, D_MODEL), F32), jax.ShapeDtypeStruct(cache_all.shape, F32)],
        grid=(n_seq // sb,),
        in_specs=in_specs,
        out_specs=[pl.BlockSpec((blk, D_MODEL), lambda i: (i, 0)), cs],
        input_output_aliases=aliases,
        compiler_params=_cparams(1),
        name="conv_step",
    )(*args)


def _conv_piece(ubuf, wdw_ref, r0, l0):
    pad = HIST_ROWS - CONV_HIST
    xs = ubuf[r0:r0 + CONV_RB + HIST_ROWS, l0:l0 + LANE]
    out = None
    for s in range(8):
        rows = CONV_RB if s == 0 else CONV_RB + 8
        z = None
        for a in range(HIST_ROWS // 8 + 1):
            j = 8 * a + s - pad
            if 0 <= j < CONV_WIDTH:
                term = wdw_ref[j:j + 1, l0:l0 + LANE] * xs[8 * a:8 * a + rows]
                z = term if z is None else z + term
        z = z[s:s + CONV_RB]
        out = z if out is None else out + z
    return out


def _merge_math(h, c_act, og, sa, sb, wpw_ref, wout_ref):
    o_conv = _dot(c_act, wpw_ref[...])
    merged = (sa * og + sb * o_conv).astype(BF16)
    return h + _dot(merged, wout_ref[...])


def _merge_body(h_ref, ca_ref, cb_ref, oga_ref, ogb_ref, sa_ref, sb_ref, wpw_ref, wout_ref, o_ref):
    c_act = jnp.concatenate([ca_ref[...], cb_ref[...]], axis=0).astype(BF16)
    og = jnp.concatenate([oga_ref[...], ogb_ref[...]], axis=0)
    o_ref[...] = _merge_math(h_ref[...], c_act, og, sa_ref[...].astype(F32), sb_ref[...].astype(F32),
                             wpw_ref, wout_ref)


def _merge(h, c_parts, og_parts, sa, sb, merge_w, layer):
    rows = h.shape[0]

    def whole(a):
        return _const_spec(a.shape)

    sq = _layer_spec((D_MODEL, D_MODEL), layer)
    args = [h, *c_parts, *og_parts, sa, sb]
    return pl.pallas_call(
        _merge_body,
        out_shape=jax.ShapeDtypeStruct((rows, D_MODEL), F32),
        grid=(1,),
        in_specs=[whole(a) for a in args] + [sq, sq],
        out_specs=pl.BlockSpec((rows, D_MODEL), lambda i: (0, 0)),
        compiler_params=_cparams(1),
        name="merge",
    )(*args, *merge_w)


def _mixer_seq_body(q_ref, k_ref, la_ref, v_ref, gs_ref, s0_ref, gn_ref, u_ref, cache_ref, h_ref, sa_ref, sb_ref,
                    wdw_ref, bdw_ref, cn_ref, wpw_ref, wout_ref, o_ref, sfin_ref, ncache_ref,
                    s_scr, og_scr, ubuf, cbuf, *, tl, chunk):
    t = pl.program_id(1)
    pad = HIST_ROWS - CONV_HIST

    @pl.when(t == 0)
    def _():
        s_scr[...] = s0_ref[...]
        ubuf[0:pad, :] = jnp.zeros((pad, D_MODEL), F32)
        ubuf[pad:HIST_ROWS, :] = cache_ref[...]

    @pl.when(t > 0)
    def _():
        ubuf[0:HIST_ROWS, :] = ubuf[tl:tl + HIST_ROWS, :]

    def store(j, h, o):
        og_scr[j * chunk:(j + 1) * chunk, h * GLA_DV:(h + 1) * GLA_DV] = o

    finals = _gla_blocks(q_ref[...], k_ref[...], v_ref[...], la_ref[...], gs_ref[...].astype(F32), gn_ref[...],
                         [s_scr[h] for h in range(GLA_HEADS)], chunk, store)
    for h in range(GLA_HEADS):
        s_scr[h] = finals[-1][h]

    ubuf[HIST_ROWS:HIST_ROWS + tl, :] = u_ref[...]
    for rb in range(tl // CONV_RB):
        rows = slice(rb * CONV_RB, (rb + 1) * CONV_RB)
        for lc in range(D_MODEL // LANE):
            lanes = slice(lc * LANE, (lc + 1) * LANE)
            cbuf[rows, lanes] = _conv_piece(ubuf, wdw_ref, rb * CONV_RB, lc * LANE) + bdw_ref[:, lanes]
    c_act = _conv_act(cbuf[...], cn_ref[...]).astype(BF16)
    o_ref[...] = _merge_math(h_ref[...], c_act, og_scr[...], sa_ref[...].astype(F32), sb_ref[...].astype(F32),
                             wpw_ref, wout_ref)

    @pl.when(t == pl.num_programs(1) - 1)
    def _():
        sfin_ref[...] = s_scr[...]
        ncache_ref[...] = ubuf[tl + pad:tl + HIST_ROWS, :]


def _mixer_seq(q, k, la, v, gs, u, h, sa, sb, s0, cache, gnorm, conv_w, merge_w, layer, tl, chunk):
    bsz, seq, _ = h.shape

    def ts(n):
        return pl.BlockSpec((None, tl, n), lambda b, t: (b, t, 0))

    vec = _layer_spec((1, D_MODEL), layer)
    sq = _layer_spec((D_MODEL, D_MODEL), layer)
    return pl.pallas_call(
        functools.partial(_mixer_seq_body, tl=tl, chunk=chunk),
        out_shape=[jax.ShapeDtypeStruct((bsz, seq, D_MODEL), F32),
                   jax.ShapeDtypeStruct((bsz,) + STATE_TAIL, F32),
                   jax.ShapeDtypeStruct((bsz, CONV_HIST, D_MODEL), F32)],
        grid=(bsz, seq // tl),
        in_specs=[ts(GLA_KEY), ts(GLA_KEY), ts(GLA_KEY), ts(GLA_VAL), ts(GLA_VAL),
                  _const_spec(STATE_TAIL), _layer_spec((1, GLA_VAL), layer),
                  ts(D_MODEL), _const_spec((CONV_HIST, D_MODEL)), ts(D_MODEL), ts(D_MODEL), ts(D_MODEL),
                  _layer_spec((CONV_WIDTH, D_MODEL), layer), vec, vec, sq, sq],
        out_specs=[ts(D_MODEL), pl.BlockSpec((None,) + STATE_TAIL, lambda b, t: (b, 0, 0, 0)),
                   pl.BlockSpec((None, CONV_HIST, D_MODEL), lambda b, t: (b, 0, 0))],
        scratch_shapes=[pltpu.VMEM(STATE_TAIL, F32), pltpu.VMEM((tl, GLA_VAL), F32),
                        pltpu.VMEM((tl + HIST_ROWS, D_MODEL), F32), pltpu.VMEM((tl, D_MODEL), F32)],
        compiler_params=_cparams(2),
        name="mixer_seq",
    )(q, k, la, v, gs, s0, gnorm, u, cache, h, sa, sb, *conv_w, *merge_w)


def kernel(x_prompt, x_sample, state_gla, cache_conv, meta_tokens, norm_ffn1, w_ffn1_gate, w_ffn1_up, w_ffn1_down, norm_mix, w_in, w_decay_up, b_decay, gla_norm, w_dw, b_dw, conv_norm, w_pw, w_out, norm_ffn2, w_ffn2_gate, w_ffn2_up, w_ffn2_down, norm_final):
    n_prompt, l_prompt, _ = x_prompt.shape
    n_sample, l_sample, _ = x_sample.shape
    rows_p = n_prompt * l_prompt
    rows_s = n_sample * l_sample
    rows_small = rows_s + N_META

    hp = x_prompt.reshape(rows_p, D_MODEL)
    hs = jnp.concatenate([x_sample.reshape(rows_s, D_MODEL), meta_tokens], axis=0)

    tm_ffn_p, tm_proj_p, tl_prompt = 512, 512, 512
    tm_proj_s = 208
    sb_gla, sb_conv = 8, 32

    def vec(a):
        return a.reshape(a.shape[0], 1, a.shape[-1])

    def bf(a):
        return a.astype(BF16)

    w_in_t = bf(jnp.swapaxes(w_in, 1, 2))
    w_du = jnp.pad(w_decay_up, ((0, 0), (0, LANE - GATE_RANK), (0, 0))).astype(BF16)

    ffn1_p = (vec(norm_ffn1), w_ffn1_gate, w_ffn1_up, w_ffn1_down)
    ffn2_p = (vec(norm_ffn2), w_ffn2_gate, w_ffn2_up, w_ffn2_down)
    proj_w = (vec(norm_mix), w_in_t, w_du, vec(b_decay))
    conv_w = (w_dw, vec(b_dw), vec(conv_norm))
    merge_w = (bf(w_pw), bf(w_out))
    gnorm = vec(gla_norm)
    gfin = norm_final.reshape(1, D_MODEL)

    zero_state = jnp.zeros((1, 1) + STATE_TAIL, F32)
    zero_cache = jnp.zeros((1, CONV_HIST, 1, D_MODEL), F32)
    cache_tm = jnp.swapaxes(cache_conv, 1, 2)

    def pseq(a):
        return a.reshape(n_prompt, l_prompt, a.shape[-1])

    sp_list, cp_list = [], []
    state_s = cache_s = None
    for l in range(DEPTH):
        last = l == DEPTH - 1
        hp, w1 = _ffn_cast(hp, *ffn1_p, l, tm_ffn_p)
        hs = _ffn(hs, (ffn1_p[0], *w1), l, rows_small)

        q, k, la, v, gs, u, sa, sb = _proj(hs, proj_w, l, tm_proj_s)
        og_s, state_s = _gla_step(q, k, la, v, gs, state_gla, gnorm, l, l, l_sample, sb_gla, 0, n_sample, state_s)
        og_m, s_m = _gla_step(q, k, la, v, gs, zero_state, gnorm, l, 0, N_META, 1, rows_s, 1, None)
        c_s, cache_s = _conv_step(u, cache_tm, conv_w, l, l, l_sample, sb_conv, 0, n_sample, cache_s)
        c_m, nc_m = _conv_step(u, zero_cache, conv_w, l, 0, N_META, 1, rows_s, 1, None)
        hs = _merge(hs, (c_s, c_m), (og_s, og_m), sa, sb, merge_w, l)

        q, k, la, v, gs, u, sa, sb = _proj(hp, proj_w, l, tm_proj_p)
        hp, s_p, nc_p = _mixer_seq(pseq(q), pseq(k), pseq(la), pseq(v), pseq(gs), pseq(u), pseq(hp), pseq(sa),
                                   pseq(sb), s_m[0, 0], nc_m[0, :, 0, :], gnorm, conv_w, merge_w, l, tl_prompt, CHUNK)
        hp, w2 = _ffn_cast(hp.reshape(rows_p, D_MODEL), *ffn2_p, l, tm_ffn_p, gfin if last else None)
        hs = _ffn(hs, (ffn2_p[0], *w2), l, rows_small, gfin if last else None)
        sp_list.append(s_p)
        cp_list.append(nc_p)

    y_prompt = hp.reshape(n_prompt, l_prompt, D_MODEL)
    y_sample = hs[:rows_s].reshape(n_sample, l_sample, D_MODEL)
    return (y_prompt, y_sample, jnp.stack(sp_list), jnp.stack(cp_list), state_s, jnp.swapaxes(cache_s, 1, 2))
```

```python
import functools

import jax
import jax.numpy as jnp
from jax import lax
from jax.experimental import pallas as pl
from jax.experimental.pallas import tpu as pltpu

F32 = jnp.float32
BF16 = jnp.bfloat16

D_MODEL = 1024
DEPTH = 4
N_META = 16
GLA_HEADS = 4
GLA_DK = 128
GLA_DV = 256
GLA_KEY = GLA_HEADS * GLA_DK
GLA_VAL = GLA_HEADS * GLA_DV
GATE_RANK = 16
GATE_TAU = 16.0
CHUNK = 64
CONV_WIDTH = 31
CONV_HIST = CONV_WIDTH - 1
D_FF = 2816
EPS = 1e-6

LANE = 128
FF_COLS = 256
HIST_ROWS = 32
CONV_RB = 128
STAGE_SLOTS = 3
STAGE_ROWS_WIDE = 128
STAGE_ROWS_TALL = 512
VMEM_LIMIT = 56 * 1024 * 1024

STATE_TAIL = (GLA_HEADS, GLA_DK, GLA_DV)
NT_DIMS = (((1,), (1,)), ((), ()))
TN_DIMS = (((0,), (0,)), ((), ()))


def _cparams(n_axes):
    return pltpu.CompilerParams(dimension_semantics=("arbitrary",) * n_axes, vmem_limit_bytes=VMEM_LIMIT)


def _const_spec(shape):
    nd = len(shape)
    return pl.BlockSpec(shape, lambda *_: (0,) * nd, pipeline_mode=pl.Buffered(1))


def _layer_spec(shape, layer):
    nd = len(shape)
    return pl.BlockSpec((None,) + tuple(shape), lambda *_: (layer,) + (0,) * nd, pipeline_mode=pl.Buffered(1))


def _alias_prev(in_specs, args, prev_out, out_index):
    if prev_out is None:
        return {}
    in_specs.append(pl.BlockSpec(memory_space=pl.ANY))
    args.append(prev_out)
    return {len(args) - 1: out_index}


def _rmsnorm(x, g):
    return x * lax.rsqrt(jnp.mean(x * x, axis=-1, keepdims=True) + EPS) * g


def _sigmoid(x):
    return 1.0 / (1.0 + jnp.exp(-x))


def _silu(x):
    return x * _sigmoid(x)


def _log_sigmoid(x):
    return jnp.minimum(x, 0.0) - jnp.log1p(jnp.exp(-jnp.abs(x)))


def _dot(a, b):
    return jnp.dot(a, b, preferred_element_type=F32)


def _ffn_math(x, g, wg_ref, wu_ref, wd_ref):
    xn = _rmsnorm(x, g).astype(BF16)
    acc = None
    for c in range(D_FF // FF_COLS):
        cols = slice(c * FF_COLS, (c + 1) * FF_COLS)
        gate = _dot(xn, wg_ref[:, cols])
        up = _dot(xn, wu_ref[:, cols])
        act = (_silu(gate) * up).astype(BF16)
        part = _dot(act, wd_ref[cols, :])
        acc = part if acc is None else acc + part
    return x + 0.5 * acc


def _ffn_body(h_ref, g_ref, wg_ref, wu_ref, wd_ref, *rest):
    o_ref = rest[-1]
    y = _ffn_math(h_ref[...], g_ref[...], wg_ref, wu_ref, wd_ref)
    if len(rest) == 2:
        y = _rmsnorm(y, rest[0][...])
    o_ref[...] = y


def _ffn(h, w, layer, tm, final_gain=None):
    rows = h.shape[0]
    row_spec = pl.BlockSpec((tm, D_MODEL), lambda i: (i, 0))
    in_specs = [row_spec, _layer_spec((1, D_MODEL), layer), _const_spec((D_MODEL, D_FF)),
                _const_spec((D_MODEL, D_FF)), _const_spec((D_FF, D_MODEL))]
    args = [h, *w]
    if final_gain is not None:
        in_specs.append(_const_spec((1, D_MODEL)))
        args.append(final_gain)
    return pl.pallas_call(
        _ffn_body,
        out_shape=jax.ShapeDtypeStruct((rows, D_MODEL), F32),
        grid=(rows // tm,),
        in_specs=in_specs,
        out_specs=row_spec,
        compiler_params=_cparams(1),
        name="ffn",
    )(*args)


def _ffn_cast_body(h_ref, g_ref, wg_hbm, wu_hbm, wd_hbm, *rest, layer, final):
    (o_ref, wg_out, wu_out, wd_out, wg_s, wu_s, wd_s, stage_a, stage_b, in_sem, out_sem) = rest[-11:]
    i = pl.program_id(0)
    rows_a, rows_b = stage_a.shape[1], stage_b.shape[1]
    n_a, n_b = D_MODEL // rows_a, D_FF // rows_b
    n_chunks = 2 * n_a + n_b

    def chunk(k):
        if k < 2 * n_a:
            hbm, dst = (wg_hbm, wg_s) if k < n_a else (wu_hbm, wu_s)
            return hbm, stage_a, dst, pl.ds((k % n_a) * rows_a, rows_a)
        return wd_hbm, stage_b, wd_s, pl.ds((k - 2 * n_a) * rows_b, rows_b)

    def chunk_copy(k):
        hbm, stage, _, rows = chunk(k)
        slot = k % STAGE_SLOTS
        return pltpu.make_async_copy(hbm.at[layer, rows, :], stage.at[slot], in_sem.at[slot])

    def write_back(j):
        src, dst = ((wg_s, wg_out), (wu_s, wu_out), (wd_s, wd_out))[j]
        return pltpu.make_async_copy(src, dst, out_sem.at[j])

    @pl.when(i == 0)
    def _():
        ahead = STAGE_SLOTS - 1
        for k in range(min(ahead, n_chunks)):
            chunk_copy(k).start()
        for k in range(n_chunks):
            if k + ahead < n_chunks:
                chunk_copy(k + ahead).start()
            chunk_copy(k).wait()
            _, stage, dst, rows = chunk(k)
            dst[rows, :] = stage[k % STAGE_SLOTS].astype(BF16)
        for j in range(3):
            write_back(j).start()

    y = _ffn_math(h_ref[...], g_ref[...], wg_s, wu_s, wd_s)
    if final:
        y = _rmsnorm(y, rest[0][...])
    o_ref[...] = y

    @pl.when(i == pl.num_programs(0) - 1)
    def _():
        for j in range(3):
            write_back(j).wait()


def _ffn_cast(h, g, wg, wu, wd, layer, tm, final_gain=None):
    rows = h.shape[0]
    assert rows // tm >= 2
    row_spec = pl.BlockSpec((tm, D_MODEL), lambda i: (i, 0))
    hbm = pl.BlockSpec(memory_space=pl.ANY)
    in_specs = [row_spec, _layer_spec((1, D_MODEL), layer), hbm, hbm, hbm]
    args = [h, g, wg, wu, wd]
    if final_gain is not None:
        in_specs.append(_const_spec((1, D_MODEL)))
        args.append(final_gain)
    outs = pl.pallas_call(
        functools.partial(_ffn_cast_body, layer=layer, final=final_gain is not None),
        out_shape=[jax.ShapeDtypeStruct((rows, D_MODEL), F32), jax.ShapeDtypeStruct((D_MODEL, D_FF), BF16),
                   jax.ShapeDtypeStruct((D_MODEL, D_FF), BF16), jax.ShapeDtypeStruct((D_FF, D_MODEL), BF16)],
        grid=(rows // tm,),
        in_specs=in_specs,
        out_specs=[row_spec, hbm, hbm, hbm],
        scratch_shapes=[pltpu.VMEM((D_MODEL, D_FF), BF16), pltpu.VMEM((D_MODEL, D_FF), BF16),
                        pltpu.VMEM((D_FF, D_MODEL), BF16),
                        pltpu.VMEM((STAGE_SLOTS, STAGE_ROWS_WIDE, D_FF), F32),
                        pltpu.VMEM((STAGE_SLOTS, FF_COLS, D_MODEL), F32),
                        pltpu.SemaphoreType.DMA((STAGE_SLOTS,)), pltpu.SemaphoreType.DMA((3,))],
        compiler_params=_cparams(1),
        name="ffn_cast",
    )(*args)
    return outs[0], tuple(outs[1:])


_OFF_Q, _OFF_K, _OFF_V, _OFF_G, _OFF_LR, _OFF_A, _OFF_GA, _OFF_MA, _OFF_MB, D_IN = (
    0, 512, 1024, 2048, 3072, 3088, 4112, 5136, 6160, 7184)


def _proj_math(h_ref, g_ref, wt_ref, wdu_ref, bdec_ref,
               q_ref, k_ref, la_ref, v_ref, gs_ref, u_ref, sa_ref, sb_ref):
    xn = _rmsnorm(h_ref[...], g_ref[...]).astype(BF16)

    def mm(lo, hi):
        return lax.dot_general(xn, wt_ref[lo:hi, :], NT_DIMS, preferred_element_type=F32)

    q_ref[...] = mm(_OFF_Q, _OFF_K) * (GLA_DK ** -0.5)
    k_ref[...] = mm(_OFF_K, _OFF_V)
    v_ref[...] = mm(_OFF_V, _OFF_G).astype(BF16)
    gs_ref[...] = _silu(mm(_OFF_G, _OFF_LR)).astype(BF16)
    u_ref[...] = mm(_OFF_A, _OFF_GA) * _sigmoid(mm(_OFF_GA, _OFF_MA))
    sa_ref[...] = _sigmoid(mm(_OFF_MA, _OFF_MB)).astype(BF16)
    sb_ref[...] = _sigmoid(mm(_OFF_MB, D_IN)).astype(BF16)
    lane = lax.broadcasted_iota(jnp.int32, (xn.shape[0], LANE), 1)
    a_lr = jnp.where(lane < GATE_RANK, mm(_OFF_LR, _OFF_LR + LANE), 0.0).astype(BF16)
    z = _dot(a_lr, wdu_ref[...]) + bdec_ref[...]
    la_ref[...] = _log_sigmoid(z) / GATE_TAU


_PROJ_OUTS = ((GLA_KEY, F32), (GLA_KEY, F32), (GLA_KEY, F32), (GLA_VAL, BF16), (GLA_VAL, BF16),
              (D_MODEL, F32), (D_MODEL, BF16), (D_MODEL, BF16))


def _proj(h, w, layer, tm):
    rows = h.shape[0]

    def rs(n):
        return pl.BlockSpec((tm, n), lambda i: (i, 0))

    return pl.pallas_call(
        _proj_math,
        out_shape=[jax.ShapeDtypeStruct((rows, n), dt) for n, dt in _PROJ_OUTS],
        grid=(rows // tm,),
        in_specs=[rs(D_MODEL), _layer_spec((1, D_MODEL), layer), _const_spec((D_IN, D_MODEL)),
                  _layer_spec((LANE, GLA_KEY), layer), _layer_spec((1, GLA_KEY), layer)],
        out_specs=[rs(n) for n, _ in _PROJ_OUTS],
        compiler_params=_cparams(1),
        name="proj",
    )(h, *w)


def _proj_cast_body(h_ref, g_ref, wt_hbm, wdu_ref, bdec_ref, *rest, layer):
    outs = rest[:len(_PROJ_OUTS)]
    wt_out, wt_s, stage, in_sem, out_sem = rest[len(_PROJ_OUTS):]
    i = pl.program_id(0)
    rows_max = stage.shape[1]
    chunks = [(r0, min(rows_max, D_IN - r0)) for r0 in range(0, D_IN, rows_max)]

    def chunk_copy(k):
        r0, n = chunks[k]
        slot = k % STAGE_SLOTS
        return pltpu.make_async_copy(wt_hbm.at[layer, pl.ds(r0, n), :], stage.at[slot, pl.ds(0, n), :],
                                     in_sem.at[slot])

    def write_back():
        return pltpu.make_async_copy(wt_s, wt_out, out_sem.at[0])

    @pl.when(i == 0)
    def _():
        ahead = STAGE_SLOTS - 1
        for k in range(min(ahead, len(chunks))):
            chunk_copy(k).start()
        for k, (r0, n) in enumerate(chunks):
            if k + ahead < len(chunks):
                chunk_copy(k + ahead).start()
            chunk_copy(k).wait()
            wt_s[r0:r0 + n, :] = stage[k % STAGE_SLOTS, 0:n, :].astype(BF16)
        write_back().start()

    _proj_math(h_ref, g_ref, wt_s, wdu_ref, bdec_ref, *outs)

    @pl.when(i == pl.num_programs(0) - 1)
    def _():
        write_back().wait()


def _proj_cast(h, g, wt, wdu, bdec, layer, tm):
    rows = h.shape[0]
    assert rows // tm >= 2

    def rs(n):
        return pl.BlockSpec((tm, n), lambda i: (i, 0))

    hbm = pl.BlockSpec(memory_space=pl.ANY)
    outs = pl.pallas_call(
        functools.partial(_proj_cast_body, layer=layer),
        out_shape=[jax.ShapeDtypeStruct((rows, n), dt) for n, dt in _PROJ_OUTS]
        + [jax.ShapeDtypeStruct((D_IN, D_MODEL), BF16)],
        grid=(rows // tm,),
        in_specs=[rs(D_MODEL), _layer_spec((1, D_MODEL), layer), hbm,
                  _layer_spec((LANE, GLA_KEY), layer), _layer_spec((1, GLA_KEY), layer)],
        out_specs=[rs(n) for n, _ in _PROJ_OUTS] + [hbm],
        scratch_shapes=[pltpu.VMEM((D_IN, D_MODEL), BF16), pltpu.VMEM((STAGE_SLOTS, STAGE_ROWS_TALL, D_MODEL), F32),
                        pltpu.SemaphoreType.DMA((STAGE_SLOTS,)), pltpu.SemaphoreType.DMA((1,))],
        compiler_params=_cparams(1),
        name="proj_cast",
    )(h, g, wt, wdu, bdec)
    return outs[:len(_PROJ_OUTS)], outs[-1]


def _chunk_cumsum(x, chunk):
    pos = lax.broadcasted_iota(jnp.int32, x.shape, 0) % chunk
    shift = 1
    while shift < chunk:
        x = x + jnp.where(pos >= shift, pltpu.roll(x, shift, 0), 0.0)
        shift *= 2
    return x


def _head_norm_gate(o, gnorm, gs):
    return o * lax.rsqrt(jnp.mean(o * o, axis=-1, keepdims=True) + EPS) * gnorm * gs


def _gla_blocks(q, k, v, la, gs, gnorm, states, blk, store):
    n_blk = q.shape[0] // blk
    heads = range(GLA_HEADS)
    kcs = [slice(h * GLA_DK, (h + 1) * GLA_DK) for h in heads]
    vcs = [slice(h * GLA_DV, (h + 1) * GLA_DV) for h in heads]
    carried = not isinstance(states[0], list)
    b_all = _chunk_cumsum(la, blk)
    q_t, k_t, k_u, v_b, dec = [], [], [], [], []
    for j in range(n_blk):
        rows = slice(j * blk, (j + 1) * blk)
        b = b_all[rows, :]
        b_last = b[blk - 1:blk, :]
        kj = k[rows, :]
        q_t.append((q[rows, :] * jnp.exp(b)).astype(BF16))
        k_t.append((kj * jnp.exp(-b)).astype(BF16))
        k_u.append((kj * jnp.exp(b_last - b)).astype(BF16))
        v_b.append(v[rows, :])
        dec.append(jnp.exp(b_last))
    tri = lax.broadcasted_iota(jnp.int32, (blk, blk), 0) >= lax.broadcasted_iota(jnp.int32, (blk, blk), 1)
    eye = (lax.broadcasted_iota(jnp.int32, (GLA_DK, GLA_DK), 0)
           == lax.broadcasted_iota(jnp.int32, (GLA_DK, GLA_DK), 1))
    probs = [[jnp.where(tri, lax.dot_general(q_t[j][:, kcs[h]], k_t[j][:, kcs[h]], NT_DIMS,
                                            preferred_element_type=F32), 0.0).astype(BF16)
              for h in heads] for j in range(n_blk)]
    upd = [[lax.dot_general(k_u[j][:, kcs[h]], v_b[j][:, vcs[h]], TN_DIMS, preferred_element_type=F32)
            for h in heads] for j in range(n_blk)]
    dec_col = [[jnp.sum(jnp.where(eye, jnp.broadcast_to(dec[j][:, kcs[h]], (GLA_DK, GLA_DK)), 0.0),
                        axis=1, keepdims=True) for h in heads] for j in range(n_blk)]
    finals = []
    cur = list(states) if carried else None
    for j in range(n_blk):
        rows = slice(j * blk, (j + 1) * blk)
        if not carried:
            cur = list(states[j])
        for h in heads:
            o = _dot(probs[j][h], v_b[j][:, vcs[h]]) + _dot(q_t[j][:, kcs[h]], cur[h].astype(BF16))
            cur[h] = dec_col[j][h] * cur[h] + upd[j][h]
            store(j, h, _head_norm_gate(o, gnorm[:, vcs[h]], gs[rows, vcs[h]]))
        finals.append(list(cur))
    return finals


def _gla_step_body(q_ref, k_ref, la_ref, v_ref, gs_ref, s0_ref, gn_ref, *rest, sb, seq):
    o_ref, sfin_ref = rest[-2:]

    def store(j, h, o):
        o_ref[j * seq:(j + 1) * seq, h * GLA_DV:(h + 1) * GLA_DV] = o.astype(o_ref.dtype)

    states = [[s0_ref[i, h] for h in range(GLA_HEADS)] for i in range(sb)]
    finals = _gla_blocks(q_ref[...], k_ref[...], v_ref[...], la_ref[...], gs_ref[...].astype(F32), gn_ref[...],
                         states, seq, store)
    for i in range(sb):
        for h in range(GLA_HEADS):
            sfin_ref[i, h] = finals[i][h]


def _gla_step(q, k, la, v, gs, s0_all, gnorm, layer, state_layer, seq, sb, row0, n_seq, prev_state):
    blk = sb * seq
    base = row0 // blk

    def bs(n, first=base):
        return pl.BlockSpec((blk, n), lambda i: (first + i, 0))

    st = pl.BlockSpec((None, sb) + STATE_TAIL, lambda i: (state_layer, i, 0, 0, 0))
    in_specs = [bs(GLA_KEY), bs(GLA_KEY), bs(GLA_KEY), bs(GLA_VAL), bs(GLA_VAL), st, _layer_spec((1, GLA_VAL), layer)]
    args = [q, k, la, v, gs, s0_all, gnorm]
    aliases = _alias_prev(in_specs, args, prev_state, 1)
    return pl.pallas_call(
        functools.partial(_gla_step_body, sb=sb, seq=seq),
        out_shape=[jax.ShapeDtypeStruct((n_seq * seq, GLA_VAL), F32), jax.ShapeDtypeStruct(s0_all.shape, F32)],
        grid=(n_seq // sb,),
        in_specs=in_specs,
        out_specs=[bs(GLA_VAL, 0), st],
        input_output_aliases=aliases,
        compiler_params=_cparams(1),
        name="gla_step",
    )(*args)


def _conv_act(acc, cn):
    return _silu(_rmsnorm(acc, cn))


def _conv_step_body(u_ref, cache_ref, wdw_ref, bdw_ref, cn_ref, *rest, sb, seq):
    c_ref, ncache_ref = rest[-2:]
    u_t = jnp.swapaxes(u_ref[...].reshape(sb, seq, D_MODEL), 0, 1)

    def up(m):
        return cache_ref[m] if m < CONV_HIST else u_t[m - CONV_HIST]

    outs = []
    for t in range(seq):
        acc = jnp.broadcast_to(bdw_ref[...], (sb, D_MODEL))
        for j in range(CONV_WIDTH):
            acc = acc + wdw_ref[j:j + 1, :] * up(t + j)
        outs.append(_conv_act(acc, cn_ref[...]))
    c_ref[...] = jnp.swapaxes(jnp.stack(outs), 0, 1).reshape(sb * seq, D_MODEL)
    for m in range(CONV_HIST):
        ncache_ref[m] = up(m + seq)


def _conv_step(u, cache_all, conv_w, layer, cache_layer, seq, sb, row0, n_seq, prev_cache):
    blk = sb * seq
    base = row0 // blk
    us = pl.BlockSpec((blk, D_MODEL), lambda i: (base + i, 0))
    cs = pl.BlockSpec((None, CONV_HIST, sb, D_MODEL), lambda i: (cache_layer, 0, i, 0))
    in_specs = [us, cs, _layer_spec((CONV_WIDTH, D_MODEL), layer), _layer_spec((1, D_MODEL), layer),
                _layer_spec((1, D_MODEL), layer)]
    args = [u, cache_all, *conv_w]
    aliases = _alias_prev(in_specs, args, prev_cache, 1)
    return pl.pallas_call(
        functools.partial(_conv_step_body, sb=sb, seq=seq),
        out_shape=[jax.ShapeDtypeStruct((n_seq * seq, D_MODEL), F32), jax.ShapeDtypeStruct(cache_all.shape, F32)],
        grid=(n_seq // sb,),
        in_specs=in_specs,
        out_specs=[pl.BlockSpec((blk, D_MODEL), lambda i: (i, 0)), cs],
        input_output_aliases=aliases,
        compiler_params=_cparams(1),
        name="conv_step",
    )(*args)


def _conv_piece(ubuf, wdw_ref, r0, l0):
    pad = HIST_ROWS - CONV_HIST
    xs = ubuf[r0:r0 + CONV_RB + HIST_ROWS, l0:l0 + LANE]
    out = None
    for s in range(8):
        rows = CONV_RB if s == 0 else CONV_RB + 8
        z = None
        for a in range(HIST_ROWS // 8 + 1):
            j = 8 * a + s - pad
            if 0 <= j < CONV_WIDTH:
                term = wdw_ref[j:j + 1, l0:l0 + LANE] * xs[8 * a:8 * a + rows]
                z = term if z is None else z + term
        z = z[s:s + CONV_RB]
        out = z if out is None else out + z
    return out


def _merge_math(h, c_act, og, sa, sb, wpw_ref, wout_ref):
    o_conv = _dot(c_act, wpw_ref[...])
    merged = (sa * og + sb * o_conv).astype(BF16)
    return h + _dot(merged, wout_ref[...])


def _merge_body(h_ref, ca_ref, cb_ref, oga_ref, ogb_ref, sa_ref, sb_ref, wpw_ref, wout_ref, o_ref):
    c_act = jnp.concatenate([ca_ref[...], cb_ref[...]], axis=0).astype(BF16)
    og = jnp.concatenate([oga_ref[...], ogb_ref[...]], axis=0)
    o_ref[...] = _merge_math(h_ref[...], c_act, og, sa_ref[...].astype(F32), sb_ref[...].astype(F32),
                             wpw_ref, wout_ref)


def _merge(h, c_parts, og_parts, sa, sb, merge_w, layer):
    rows = h.shape[0]

    def whole(a):
        return _const_spec(a.shape)

    sq = _layer_spec((D_MODEL, D_MODEL), layer)
    args = [h, *c_parts, *og_parts, sa, sb]
    return pl.pallas_call(
        _merge_body,
        out_shape=jax.ShapeDtypeStruct((rows, D_MODEL), F32),
        grid=(1,),
        in_specs=[whole(a) for a in args] + [sq, sq],
        out_specs=pl.BlockSpec((rows, D_MODEL), lambda i: (0, 0)),
        compiler_params=_cparams(1),
        name="merge",
    )(*args, *merge_w)


def _mixer_seq_body(q_ref, k_ref, la_ref, v_ref, gs_ref, s0_ref, gn_ref, u_ref, cache_ref, h_ref, sa_ref, sb_ref,
                    wdw_ref, bdw_ref, cn_ref, wpw_ref, wout_ref, o_ref, sfin_ref, ncache_ref,
                    s_scr, og_scr, ubuf, cbuf, *, tl, chunk):
    t = pl.program_id(1)
    pad = HIST_ROWS - CONV_HIST

    @pl.when(t == 0)
    def _():
        s_scr[...] = s0_ref[...]
        ubuf[0:pad, :] = jnp.zeros((pad, D_MODEL), F32)
        ubuf[pad:HIST_ROWS, :] = cache_ref[...]

    @pl.when(t > 0)
    def _():
        ubuf[0:HIST_ROWS, :] = ubuf[tl:tl + HIST_ROWS, :]

    def store(j, h, o):
        og_scr[j * chunk:(j + 1) * chunk, h * GLA_DV:(h + 1) * GLA_DV] = o

    finals = _gla_blocks(q_ref[...], k_ref[...], v_ref[...], la_ref[...], gs_ref[...].astype(F32), gn_ref[...],
                         [s_scr[h] for h in range(GLA_HEADS)], chunk, store)
    for h in range(GLA_HEADS):
        s_scr[h] = finals[-1][h]

    ubuf[HIST_ROWS:HIST_ROWS + tl, :] = u_ref[...]
    for rb in range(tl // CONV_RB):
        rows = slice(rb * CONV_RB, (rb + 1) * CONV_RB)
        for lc in range(D_MODEL // LANE):
            lanes = slice(lc * LANE, (lc + 1) * LANE)
            cbuf[rows, lanes] = _conv_piece(ubuf, wdw_ref, rb * CONV_RB, lc * LANE) + bdw_ref[:, lanes]
    c_act = _conv_act(cbuf[...], cn_ref[...]).astype(BF16)
    o_ref[...] = _merge_math(h_ref[...], c_act, og_scr[...], sa_ref[...].astype(F32), sb_ref[...].astype(F32),
                             wpw_ref, wout_ref)

    @pl.when(t == pl.num_programs(1) - 1)
    def _():
        sfin_ref[...] = s_scr[...]
        ncache_ref[...] = ubuf[tl + pad:tl + HIST_ROWS, :]


def _mixer_seq(q, k, la, v, gs, u, h, sa, sb, s0, cache, gnorm, conv_w, merge_w, layer, tl, chunk):
    bsz, seq, _ = h.shape

    def ts(n):
        return pl.BlockSpec((None, tl, n), lambda b, t: (b, t, 0))

    vec = _layer_spec((1, D_MODEL), layer)
    sq = _layer_spec((D_MODEL, D_MODEL), layer)
    return pl.pallas_call(
        functools.partial(_mixer_seq_body, tl=tl, chunk=chunk),
        out_shape=[jax.ShapeDtypeStruct((bsz, seq, D_MODEL), F32),
                   jax.ShapeDtypeStruct((bsz,) + STATE_TAIL, F32),
                   jax.ShapeDtypeStruct((bsz, CONV_HIST, D_MODEL), F32)],
        grid=(bsz, seq // tl),
        in_specs=[ts(GLA_KEY), ts(GLA_KEY), ts(GLA_KEY), ts(GLA_VAL), ts(GLA_VAL),
                  _const_spec(STATE_TAIL), _layer_spec((1, GLA_VAL), layer),
                  ts(D_MODEL), _const_spec((CONV_HIST, D_MODEL)), ts(D_MODEL), ts(D_MODEL), ts(D_MODEL),
                  _layer_spec((CONV_WIDTH, D_MODEL), layer), vec, vec, sq, sq],
        out_specs=[ts(D_MODEL), pl.BlockSpec((None,) + STATE_TAIL, lambda b, t: (b, 0, 0, 0)),
                   pl.BlockSpec((None, CONV_HIST, D_MODEL), lambda b, t: (b, 0, 0))],
        scratch_shapes=[pltpu.VMEM(STATE_TAIL, F32), pltpu.VMEM((tl, GLA_VAL), F32),
                        pltpu.VMEM((tl + HIST_ROWS, D_MODEL), F32), pltpu.VMEM((tl, D_MODEL), F32)],
        compiler_params=_cparams(2),
        name="mixer_seq",
    )(q, k, la, v, gs, s0, gnorm, u, cache, h, sa, sb, *conv_w, *merge_w)


def kernel(x_prompt, x_sample, state_gla, cache_conv, meta_tokens, norm_ffn1, w_ffn1_gate, w_ffn1_up, w_ffn1_down, norm_mix, w_in, w_decay_up, b_decay, gla_norm, w_dw, b_dw, conv_norm, w_pw, w_out, norm_ffn2, w_ffn2_gate, w_ffn2_up, w_ffn2_down, norm_final):
    n_prompt, l_prompt, _ = x_prompt.shape
    n_sample, l_sample, _ = x_sample.shape
    rows_p = n_prompt * l_prompt
    rows_s = n_sample * l_sample
    rows_small = rows_s + N_META

    hp = x_prompt.reshape(rows_p, D_MODEL)
    hs = jnp.concatenate([x_sample.reshape(rows_s, D_MODEL), meta_tokens], axis=0)

    tm_ffn_p, tm_proj_p, tl_prompt = 512, 512, 512
    tm_proj_s = 208
    sb_gla, sb_conv = 8, 32

    def vec(a):
        return a.reshape(a.shape[0], 1, a.shape[-1])

    def bf(a):
        return a.astype(BF16)

    w_in_t = jnp.swapaxes(w_in, 1, 2)
    w_du = jnp.pad(w_decay_up, ((0, 0), (0, LANE - GATE_RANK), (0, 0))).astype(BF16)

    ffn1_p = (vec(norm_ffn1), w_ffn1_gate, w_ffn1_up, w_ffn1_down)
    ffn2_p = (vec(norm_ffn2), w_ffn2_gate, w_ffn2_up, w_ffn2_down)
    g_mix, b_dec = vec(norm_mix), vec(b_decay)
    conv_w = (w_dw, vec(b_dw), vec(conv_norm))
    merge_w = (bf(w_pw), bf(w_out))
    gnorm = vec(gla_norm)
    gfin = norm_final.reshape(1, D_MODEL)

    zero_state = jnp.zeros((1, 1) + STATE_TAIL, F32)
    zero_cache = jnp.zeros((1, CONV_HIST, 1, D_MODEL), F32)
    cache_tm = jnp.swapaxes(cache_conv, 1, 2)

    def pseq(a):
        return a.reshape(n_prompt, l_prompt, a.shape[-1])

    sp_list, cp_list = [], []
    state_s = cache_s = None
    for l in range(DEPTH):
        last = l == DEPTH - 1
        hp, w1 = _ffn_cast(hp, *ffn1_p, l, tm_ffn_p)
        hs = _ffn(hs, (ffn1_p[0], *w1), l, rows_small)

        (q, k, la, v, gs, u, sa, sb), w_in_l = _proj_cast(hs, g_mix, w_in_t, w_du, b_dec, l, tm_proj_s)
        og_s, state_s = _gla_step(q, k, la, v, gs, state_gla, gnorm, l, l, l_sample, sb_gla, 0, n_sample, state_s)
        og_m, s_m = _gla_step(q, k, la, v, gs, zero_state, gnorm, l, 0, N_META, 1, rows_s, 1, None)
        c_s, cache_s = _conv_step(u, cache_tm, conv_w, l, l, l_sample, sb_conv, 0, n_sample, cache_s)
        c_m, nc_m = _conv_step(u, zero_cache, conv_w, l, 0, N_META, 1, rows_s, 1, None)
        hs = _merge(hs, (c_s, c_m), (og_s, og_m), sa, sb, merge_w, l)

        q, k, la, v, gs, u, sa, sb = _proj(hp, (g_mix, w_in_l, w_du, b_dec), l, tm_proj_p)
        hp, s_p, nc_p = _mixer_seq(pseq(q), pseq(k), pseq(la), pseq(v), pseq(gs), pseq(u), pseq(hp), pseq(sa),
                                   pseq(sb), s_m[0, 0], nc_m[0, :, 0, :], gnorm, conv_w, merge_w, l, tl_prompt, CHUNK)
        hp, w2 = _ffn_cast(hp.reshape(rows_p, D_MODEL), *ffn2_p, l, tm_ffn_p, gfin if last else None)
        hs = _ffn(hs, (ffn2_p[0], *w2), l, rows_small, gfin if last else None)
        sp_list.append(s_p)
        cp_list.append(nc_p)

    y_prompt = hp.reshape(n_prompt, l_prompt, D_MODEL)
    y_sample = hs[:rows_s].reshape(n_sample, l_sample, D_MODEL)
    return (y_prompt, y_sample, jnp.stack(sp_list), jnp.stack(cp_list), state_s, jnp.swapaxes(cache_s, 1, 2))
```

```python
import functools

import jax
import jax.numpy as jnp
from jax import lax
from jax.experimental import pallas as pl
from jax.experimental.pallas import tpu as pltpu

F32 = jnp.float32
BF16 = jnp.bfloat16

D_MODEL = 1024
DEPTH = 4
N_META = 16
GLA_HEADS = 4
GLA_DK = 128
GLA_DV = 256
GLA_KEY = GLA_HEADS * GLA_DK
GLA_VAL = GLA_HEADS * GLA_DV
GATE_RANK = 16
GATE_TAU = 16.0
CHUNK = 64
CONV_WIDTH = 31
CONV_HIST = CONV_WIDTH - 1
D_FF = 2816
EPS = 1e-6

LANE = 128
FF_COLS = 256
HIST_ROWS = 32
CONV_RB = 128
MIX_PARTS = 2
STAGE_SLOTS = 3
STAGE_ROWS_WIDE = 128
STAGE_ROWS_TALL = 512
VMEM_LIMIT = 56 * 1024 * 1024

STATE_TAIL = (GLA_HEADS, GLA_DK, GLA_DV)
NT_DIMS = (((1,), (1,)), ((), ()))
TN_DIMS = (((0,), (0,)), ((), ()))


def _cparams(n_axes):
    return pltpu.CompilerParams(dimension_semantics=("arbitrary",) * n_axes, vmem_limit_bytes=VMEM_LIMIT)


def _const_spec(shape):
    nd = len(shape)
    return pl.BlockSpec(shape, lambda *_: (0,) * nd, pipeline_mode=pl.Buffered(1))


def _layer_spec(shape, layer):
    nd = len(shape)
    return pl.BlockSpec((None,) + tuple(shape), lambda *_: (layer,) + (0,) * nd, pipeline_mode=pl.Buffered(1))


def _alias_prev(in_specs, args, prev_out, out_index):
    if prev_out is None:
        return {}
    in_specs.append(pl.BlockSpec(memory_space=pl.ANY))
    args.append(prev_out)
    return {len(args) - 1: out_index}


def _rmsnorm(x, g):
    return x * lax.rsqrt(jnp.mean(x * x, axis=-1, keepdims=True) + EPS) * g


def _sigmoid(x):
    return 1.0 / (1.0 + jnp.exp(-x))


def _silu(x):
    return x * _sigmoid(x)


def _log_sigmoid(x):
    return jnp.minimum(x, 0.0) - jnp.log1p(jnp.exp(-jnp.abs(x)))


def _dot(a, b):
    return jnp.dot(a, b, preferred_element_type=F32)


def _ffn_math(x, g, wg_ref, wu_ref, wd_ref):
    xn = _rmsnorm(x, g).astype(BF16)
    acc = None
    for c in range(D_FF // FF_COLS):
        cols = slice(c * FF_COLS, (c + 1) * FF_COLS)
        gate = _dot(xn, wg_ref[:, cols])
        up = _dot(xn, wu_ref[:, cols])
        act = (_silu(gate) * up).astype(BF16)
        part = _dot(act, wd_ref[cols, :])
        acc = part if acc is None else acc + part
    return x + 0.5 * acc


def _ffn_body(h_ref, g_ref, wg_ref, wu_ref, wd_ref, *rest):
    o_ref = rest[-1]
    y = _ffn_math(h_ref[...], g_ref[...], wg_ref, wu_ref, wd_ref)
    if len(rest) == 2:
        y = _rmsnorm(y, rest[0][...])
    o_ref[...] = y


def _ffn(h, w, layer, tm, final_gain=None):
    rows = h.shape[0]
    row_spec = pl.BlockSpec((tm, D_MODEL), lambda i: (i, 0))
    in_specs = [row_spec, _layer_spec((1, D_MODEL), layer), _const_spec((D_MODEL, D_FF)),
                _const_spec((D_MODEL, D_FF)), _const_spec((D_FF, D_MODEL))]
    args = [h, *w]
    if final_gain is not None:
        in_specs.append(_const_spec((1, D_MODEL)))
        args.append(final_gain)
    return pl.pallas_call(
        _ffn_body,
        out_shape=jax.ShapeDtypeStruct((rows, D_MODEL), F32),
        grid=(rows // tm,),
        in_specs=in_specs,
        out_specs=row_spec,
        compiler_params=_cparams(1),
        name="ffn",
    )(*args)


def _ffn_cast_body(h_ref, g_ref, wg_hbm, wu_hbm, wd_hbm, *rest, layer, final):
    (o_ref, wg_out, wu_out, wd_out, wg_s, wu_s, wd_s, stage_a, stage_b, in_sem, out_sem) = rest[-11:]
    i = pl.program_id(0)
    rows_a, rows_b = stage_a.shape[1], stage_b.shape[1]
    n_a, n_b = D_MODEL // rows_a, D_FF // rows_b
    n_chunks = 2 * n_a + n_b

    def chunk(k):
        if k < 2 * n_a:
            hbm, dst = (wg_hbm, wg_s) if k < n_a else (wu_hbm, wu_s)
            return hbm, stage_a, dst, pl.ds((k % n_a) * rows_a, rows_a)
        return wd_hbm, stage_b, wd_s, pl.ds((k - 2 * n_a) * rows_b, rows_b)

    def chunk_copy(k):
        hbm, stage, _, rows = chunk(k)
        slot = k % STAGE_SLOTS
        return pltpu.make_async_copy(hbm.at[layer, rows, :], stage.at[slot], in_sem.at[slot])

    def write_back(j):
        src, dst = ((wg_s, wg_out), (wu_s, wu_out), (wd_s, wd_out))[j]
        return pltpu.make_async_copy(src, dst, out_sem.at[j])

    @pl.when(i == 0)
    def _():
        ahead = STAGE_SLOTS - 1
        for k in range(min(ahead, n_chunks)):
            chunk_copy(k).start()
        for k in range(n_chunks):
            if k + ahead < n_chunks:
                chunk_copy(k + ahead).start()
            chunk_copy(k).wait()
            _, stage, dst, rows = chunk(k)
            dst[rows, :] = stage[k % STAGE_SLOTS].astype(BF16)
        for j in range(3):
            write_back(j).start()

    y = _ffn_math(h_ref[...], g_ref[...], wg_s, wu_s, wd_s)
    if final:
        y = _rmsnorm(y, rest[0][...])
    o_ref[...] = y

    @pl.when(i == pl.num_programs(0) - 1)
    def _():
        for j in range(3):
            write_back(j).wait()


def _ffn_cast(h, g, wg, wu, wd, layer, tm, final_gain=None):
    rows = h.shape[0]
    assert rows // tm >= 2
    row_spec = pl.BlockSpec((tm, D_MODEL), lambda i: (i, 0))
    hbm = pl.BlockSpec(memory_space=pl.ANY)
    in_specs = [row_spec, _layer_spec((1, D_MODEL), layer), hbm, hbm, hbm]
    args = [h, g, wg, wu, wd]
    if final_gain is not None:
        in_specs.append(_const_spec((1, D_MODEL)))
        args.append(final_gain)
    outs = pl.pallas_call(
        functools.partial(_ffn_cast_body, layer=layer, final=final_gain is not None),
        out_shape=[jax.ShapeDtypeStruct((rows, D_MODEL), F32), jax.ShapeDtypeStruct((D_MODEL, D_FF), BF16),
                   jax.ShapeDtypeStruct((D_MODEL, D_FF), BF16), jax.ShapeDtypeStruct((D_FF, D_MODEL), BF16)],
        grid=(rows // tm,),
        in_specs=in_specs,
        out_specs=[row_spec, hbm, hbm, hbm],
        scratch_shapes=[pltpu.VMEM((D_MODEL, D_FF), BF16), pltpu.VMEM((D_MODEL, D_FF), BF16),
                        pltpu.VMEM((D_FF, D_MODEL), BF16),
                        pltpu.VMEM((STAGE_SLOTS, STAGE_ROWS_WIDE, D_FF), F32),
                        pltpu.VMEM((STAGE_SLOTS, FF_COLS, D_MODEL), F32),
                        pltpu.SemaphoreType.DMA((STAGE_SLOTS,)), pltpu.SemaphoreType.DMA((3,))],
        compiler_params=_cparams(1),
        name="ffn_cast",
    )(*args)
    return outs[0], tuple(outs[1:])


_OFF_Q, _OFF_K, _OFF_V, _OFF_G, _OFF_LR, _OFF_A, _OFF_GA, _OFF_MA, _OFF_MB, D_IN = (
    0, 512, 1024, 2048, 3072, 3088, 4112, 5136, 6160, 7184)


def _proj_math(h_ref, g_ref, wt_ref, wdu_ref, bdec_ref,
               q_ref, k_ref, la_ref, v_ref, gs_ref, u_ref, sa_ref, sb_ref):
    xn = _rmsnorm(h_ref[...], g_ref[...]).astype(BF16)

    def mm(lo, hi):
        return lax.dot_general(xn, wt_ref[lo:hi, :], NT_DIMS, preferred_element_type=F32)

    q_ref[...] = mm(_OFF_Q, _OFF_K) * (GLA_DK ** -0.5)
    k_ref[...] = mm(_OFF_K, _OFF_V)
    v_ref[...] = mm(_OFF_V, _OFF_G).astype(BF16)
    gs_ref[...] = _silu(mm(_OFF_G, _OFF_LR)).astype(BF16)
    u_ref[...] = mm(_OFF_A, _OFF_GA) * _sigmoid(mm(_OFF_GA, _OFF_MA))
    sa_ref[...] = _sigmoid(mm(_OFF_MA, _OFF_MB)).astype(BF16)
    sb_ref[...] = _sigmoid(mm(_OFF_MB, D_IN)).astype(BF16)
    lane = lax.broadcasted_iota(jnp.int32, (xn.shape[0], LANE), 1)
    a_lr = jnp.where(lane < GATE_RANK, mm(_OFF_LR, _OFF_LR + LANE), 0.0).astype(BF16)
    z = _dot(a_lr, wdu_ref[...]) + bdec_ref[...]
    la_ref[...] = _log_sigmoid(z) / GATE_TAU


_PROJ_OUTS = ((GLA_KEY, F32), (GLA_KEY, F32), (GLA_KEY, F32), (GLA_VAL, BF16), (GLA_VAL, BF16),
              (D_MODEL, F32), (D_MODEL, BF16), (D_MODEL, BF16))


def _proj(h, w, layer, tm):
    rows = h.shape[0]

    def rs(n):
        return pl.BlockSpec((tm, n), lambda i: (i, 0))

    return pl.pallas_call(
        _proj_math,
        out_shape=[jax.ShapeDtypeStruct((rows, n), dt) for n, dt in _PROJ_OUTS],
        grid=(rows // tm,),
        in_specs=[rs(D_MODEL), _layer_spec((1, D_MODEL), layer), _const_spec((D_IN, D_MODEL)),
                  _layer_spec((LANE, GLA_KEY), layer), _layer_spec((1, GLA_KEY), layer)],
        out_specs=[rs(n) for n, _ in _PROJ_OUTS],
        compiler_params=_cparams(1),
        name="proj",
    )(h, *w)


def _proj_cast_body(h_ref, g_ref, wt_hbm, wdu_ref, bdec_ref, *rest, layer):
    outs = rest[:len(_PROJ_OUTS)]
    wt_out, wt_s, stage, in_sem, out_sem = rest[len(_PROJ_OUTS):]
    i = pl.program_id(0)
    rows_max = stage.shape[1]
    chunks = [(r0, min(rows_max, D_IN - r0)) for r0 in range(0, D_IN, rows_max)]

    def chunk_copy(k):
        r0, n = chunks[k]
        slot = k % STAGE_SLOTS
        return pltpu.make_async_copy(wt_hbm.at[layer, pl.ds(r0, n), :], stage.at[slot, pl.ds(0, n), :],
                                     in_sem.at[slot])

    def write_back():
        return pltpu.make_async_copy(wt_s, wt_out, out_sem.at[0])

    @pl.when(i == 0)
    def _():
        ahead = STAGE_SLOTS - 1
        for k in range(min(ahead, len(chunks))):
            chunk_copy(k).start()
        for k, (r0, n) in enumerate(chunks):
            if k + ahead < len(chunks):
                chunk_copy(k + ahead).start()
            chunk_copy(k).wait()
            wt_s[r0:r0 + n, :] = stage[k % STAGE_SLOTS, 0:n, :].astype(BF16)
        write_back().start()

    _proj_math(h_ref, g_ref, wt_s, wdu_ref, bdec_ref, *outs)

    @pl.when(i == pl.num_programs(0) - 1)
    def _():
        write_back().wait()


def _proj_cast(h, g, wt, wdu, bdec, layer, tm):
    rows = h.shape[0]
    assert rows // tm >= 2

    def rs(n):
        return pl.BlockSpec((tm, n), lambda i: (i, 0))

    hbm = pl.BlockSpec(memory_space=pl.ANY)
    outs = pl.pallas_call(
        functools.partial(_proj_cast_body, layer=layer),
        out_shape=[jax.ShapeDtypeStruct((rows, n), dt) for n, dt in _PROJ_OUTS]
        + [jax.ShapeDtypeStruct((D_IN, D_MODEL), BF16)],
        grid=(rows // tm,),
        in_specs=[rs(D_MODEL), _layer_spec((1, D_MODEL), layer), hbm,
                  _layer_spec((LANE, GLA_KEY), layer), _layer_spec((1, GLA_KEY), layer)],
        out_specs=[rs(n) for n, _ in _PROJ_OUTS] + [hbm],
        scratch_shapes=[pltpu.VMEM((D_IN, D_MODEL), BF16), pltpu.VMEM((STAGE_SLOTS, STAGE_ROWS_TALL, D_MODEL), F32),
                        pltpu.SemaphoreType.DMA((STAGE_SLOTS,)), pltpu.SemaphoreType.DMA((1,))],
        compiler_params=_cparams(1),
        name="proj_cast",
    )(h, g, wt, wdu, bdec)
    return outs[:len(_PROJ_OUTS)], outs[-1]


def _chunk_cumsum(x, chunk):
    pos = lax.broadcasted_iota(jnp.int32, x.shape, 0) % chunk
    shift = 1
    while shift < chunk:
        x = x + jnp.where(pos >= shift, pltpu.roll(x, shift, 0), 0.0)
        shift *= 2
    return x


def _head_norm_gate(o, gnorm, gs):
    return o * lax.rsqrt(jnp.mean(o * o, axis=-1, keepdims=True) + EPS) * gnorm * gs


def _gla_blocks(q, k, v, la, gs, gnorm, states, blk, store):
    n_blk = q.shape[0] // blk
    heads = range(GLA_HEADS)
    kcs = [slice(h * GLA_DK, (h + 1) * GLA_DK) for h in heads]
    vcs = [slice(h * GLA_DV, (h + 1) * GLA_DV) for h in heads]
    carried = not isinstance(states[0], list)
    b_all = _chunk_cumsum(la, blk)
    q_t, k_t, k_u, v_b, dec = [], [], [], [], []
    for j in range(n_blk):
        rows = slice(j * blk, (j + 1) * blk)
        b = b_all[rows, :]
        b_last = b[blk - 1:blk, :]
        kj = k[rows, :]
        q_t.append((q[rows, :] * jnp.exp(b)).astype(BF16))
        k_t.append((kj * jnp.exp(-b)).astype(BF16))
        k_u.append((kj * jnp.exp(b_last - b)).astype(BF16))
        v_b.append(v[rows, :])
        dec.append(jnp.exp(b_last))
    tri = lax.broadcasted_iota(jnp.int32, (blk, blk), 0) >= lax.broadcasted_iota(jnp.int32, (blk, blk), 1)
    eye = (lax.broadcasted_iota(jnp.int32, (GLA_DK, GLA_DK), 0)
           == lax.broadcasted_iota(jnp.int32, (GLA_DK, GLA_DK), 1))
    probs = [[jnp.where(tri, lax.dot_general(q_t[j][:, kcs[h]], k_t[j][:, kcs[h]], NT_DIMS,
                                            preferred_element_type=F32), 0.0).astype(BF16)
              for h in heads] for j in range(n_blk)]
    upd = [[lax.dot_general(k_u[j][:, kcs[h]], v_b[j][:, vcs[h]], TN_DIMS, preferred_element_type=F32)
            for h in heads] for j in range(n_blk)]
    dec_col = [[jnp.sum(jnp.where(eye, jnp.broadcast_to(dec[j][:, kcs[h]], (GLA_DK, GLA_DK)), 0.0),
                        axis=1, keepdims=True) for h in heads] for j in range(n_blk)]
    finals = []
    cur = list(states) if carried else None
    for j in range(n_blk):
        rows = slice(j * blk, (j + 1) * blk)
        if not carried:
            cur = list(states[j])
        for h in heads:
            o = _dot(probs[j][h], v_b[j][:, vcs[h]]) + _dot(q_t[j][:, kcs[h]], cur[h].astype(BF16))
            cur[h] = dec_col[j][h] * cur[h] + upd[j][h]
            store(j, h, _head_norm_gate(o, gnorm[:, vcs[h]], gs[rows, vcs[h]]))
        finals.append(list(cur))
    return finals


def _gla_step_body(q_ref, k_ref, la_ref, v_ref, gs_ref, s0_ref, gn_ref, *rest, sb, seq):
    o_ref, sfin_ref = rest[-2:]

    def store(j, h, o):
        o_ref[j * seq:(j + 1) * seq, h * GLA_DV:(h + 1) * GLA_DV] = o.astype(o_ref.dtype)

    states = [[s0_ref[i, h] for h in range(GLA_HEADS)] for i in range(sb)]
    finals = _gla_blocks(q_ref[...], k_ref[...], v_ref[...], la_ref[...], gs_ref[...].astype(F32), gn_ref[...],
                         states, seq, store)
    for i in range(sb):
        for h in range(GLA_HEADS):
            sfin_ref[i, h] = finals[i][h]


def _gla_step(q, k, la, v, gs, s0_all, gnorm, layer, state_layer, seq, sb, row0, n_seq, prev_state):
    blk = sb * seq
    base = row0 // blk

    def bs(n, first=base):
        return pl.BlockSpec((blk, n), lambda i: (first + i, 0))

    st = pl.BlockSpec((None, sb) + STATE_TAIL, lambda i: (state_layer, i, 0, 0, 0))
    in_specs = [bs(GLA_KEY), bs(GLA_KEY), bs(GLA_KEY), bs(GLA_VAL), bs(GLA_VAL), st, _layer_spec((1, GLA_VAL), layer)]
    args = [q, k, la, v, gs, s0_all, gnorm]
    aliases = _alias_prev(in_specs, args, prev_state, 1)
    return pl.pallas_call(
        functools.partial(_gla_step_body, sb=sb, seq=seq),
        out_shape=[jax.ShapeDtypeStruct((n_seq * seq, GLA_VAL), F32), jax.ShapeDtypeStruct(s0_all.shape, F32)],
        grid=(n_seq // sb,),
        in_specs=in_specs,
        out_specs=[bs(GLA_VAL, 0), st],
        input_output_aliases=aliases,
        compiler_params=_cparams(1),
        name="gla_step",
    )(*args)


def _conv_act(acc, cn):
    return _silu(_rmsnorm(acc, cn))


def _conv_step_body(u_ref, cache_ref, wdw_ref, bdw_ref, cn_ref, *rest, sb, seq):
    c_ref, ncache_ref = rest[-2:]
    u_t = jnp.swapaxes(u_ref[...].reshape(sb, seq, D_MODEL), 0, 1)

    def up(m):
        return cache_ref[m] if m < CONV_HIST else u_t[m - CONV_HIST]

    outs = []
    for t in range(seq):
        acc = jnp.broadcast_to(bdw_ref[...], (sb, D_MODEL))
        for j in range(CONV_WIDTH):
            acc = acc + wdw_ref[j:j + 1, :] * up(t + j)
        outs.append(_conv_act(acc, cn_ref[...]))
    c_ref[...] = jnp.swapaxes(jnp.stack(outs), 0, 1).reshape(sb * seq, D_MODEL)
    for m in range(CONV_HIST):
        ncache_ref[m] = up(m + seq)


def _conv_step(u, cache_all, conv_w, layer, cache_layer, seq, sb, row0, n_seq, prev_cache):
    blk = sb * seq
    base = row0 // blk
    us = pl.BlockSpec((blk, D_MODEL), lambda i: (base + i, 0))
    cs = pl.BlockSpec((None, CONV_HIST, sb, D_MODEL), lambda i: (cache_layer, 0, i, 0))
    in_specs = [us, cs, _layer_spec((CONV_WIDTH, D_MODEL), layer), _layer_spec((1, D_MODEL), layer),
                _layer_spec((1, D_MODEL), layer)]
    args = [u, cache_all, *conv_w]
    aliases = _alias_prev(in_specs, args, prev_cache, 1)
    return pl.pallas_call(
        functools.partial(_conv_step_body, sb=sb, seq=seq),
        out_shape=[jax.ShapeDtypeStruct((n_seq * seq, D_MODEL), F32), jax.ShapeDtypeStruct(cache_all.shape, F32)],
        grid=(n_seq // sb,),
        in_specs=in_specs,
        out_specs=[pl.BlockSpec((blk, D_MODEL), lambda i: (i, 0)), cs],
        input_output_aliases=aliases,
        compiler_params=_cparams(1),
        name="conv_step",
    )(*args)


def _conv_piece(ubuf, wdw_ref, r0, l0):
    pad = HIST_ROWS - CONV_HIST
    xs = ubuf[r0:r0 + CONV_RB + HIST_ROWS, l0:l0 + LANE]
    out = None
    for s in range(8):
        rows = CONV_RB if s == 0 else CONV_RB + 8
        z = None
        for a in range(HIST_ROWS // 8 + 1):
            j = 8 * a + s - pad
            if 0 <= j < CONV_WIDTH:
                term = wdw_ref[j:j + 1, l0:l0 + LANE] * xs[8 * a:8 * a + rows]
                z = term if z is None else z + term
        z = z[s:s + CONV_RB]
        out = z if out is None else out + z
    return out


def _merge_math(h, c_act, og, sa, sb, wpw_ref, wout_ref):
    o_conv = _dot(c_act, wpw_ref[...])
    merged = (sa * og + sb * o_conv).astype(BF16)
    return h + _dot(merged, wout_ref[...])


def _merge_body(h_ref, ca_ref, cb_ref, oga_ref, ogb_ref, sa_ref, sb_ref, wpw_ref, wout_ref, o_ref):
    c_act = jnp.concatenate([ca_ref[...], cb_ref[...]], axis=0).astype(BF16)
    og = jnp.concatenate([oga_ref[...], ogb_ref[...]], axis=0)
    o_ref[...] = _merge_math(h_ref[...], c_act, og, sa_ref[...].astype(F32), sb_ref[...].astype(F32),
                             wpw_ref, wout_ref)


def _merge(h, c_parts, og_parts, sa, sb, merge_w, layer):
    rows = h.shape[0]

    def whole(a):
        return _const_spec(a.shape)

    sq = _layer_spec((D_MODEL, D_MODEL), layer)
    args = [h, *c_parts, *og_parts, sa, sb]
    return pl.pallas_call(
        _merge_body,
        out_shape=jax.ShapeDtypeStruct((rows, D_MODEL), F32),
        grid=(1,),
        in_specs=[whole(a) for a in args] + [sq, sq],
        out_specs=pl.BlockSpec((rows, D_MODEL), lambda i: (0, 0)),
        compiler_params=_cparams(1),
        name="merge",
    )(*args, *merge_w)


def _mixer_seq_body(q_ref, k_ref, la_ref, v_ref, gs_ref, s0_ref, gn_ref, u_ref, cache_ref, h_ref, sa_ref, sb_ref,
                    wdw_ref, bdw_ref, cn_ref, wpw_ref, wout_ref, o_ref, sfin_ref, ncache_ref,
                    s_scr, og_scr, ubuf, cbuf, *, tl, chunk):
    t = pl.program_id(1)
    pad = HIST_ROWS - CONV_HIST

    @pl.when(t == 0)
    def _():
        s_scr[...] = s0_ref[...]
        ubuf[0:pad, :] = jnp.zeros((pad, D_MODEL), F32)
        ubuf[pad:HIST_ROWS, :] = cache_ref[...]

    @pl.when(t > 0)
    def _():
        ubuf[0:HIST_ROWS, :] = ubuf[tl:tl + HIST_ROWS, :]

    def store(j, h, o):
        og_scr[j * chunk:(j + 1) * chunk, h * GLA_DV:(h + 1) * GLA_DV] = o

    finals = _gla_blocks(q_ref[...], k_ref[...], v_ref[...], la_ref[...], gs_ref[...].astype(F32), gn_ref[...],
                         [s_scr[h] for h in range(GLA_HEADS)], chunk, store)
    for h in range(GLA_HEADS):
        s_scr[h] = finals[-1][h]

    ubuf[HIST_ROWS:HIST_ROWS + tl, :] = u_ref[...]
    part = tl // MIX_PARTS
    for p in range(MIX_PARTS):
        prow = slice(p * part, (p + 1) * part)
        for rb in range(p * part // CONV_RB, (p + 1) * part // CONV_RB):
            rows = slice(rb * CONV_RB, (rb + 1) * CONV_RB)
            for lc in range(D_MODEL // LANE):
                lanes = slice(lc * LANE, (lc + 1) * LANE)
                cbuf[rows, lanes] = _conv_piece(ubuf, wdw_ref, rb * CONV_RB, lc * LANE) + bdw_ref[:, lanes]
        c_act = _conv_act(cbuf[prow, :], cn_ref[...]).astype(BF16)
        o_ref[prow, :] = _merge_math(h_ref[prow, :], c_act, og_scr[prow, :], sa_ref[prow, :].astype(F32),
                                     sb_ref[prow, :].astype(F32), wpw_ref, wout_ref)

    @pl.when(t == pl.num_programs(1) - 1)
    def _():
        sfin_ref[...] = s_scr[...]
        ncache_ref[...] = ubuf[tl + pad:tl + HIST_ROWS, :]


def _mixer_seq(q, k, la, v, gs, u, h, sa, sb, s0, cache, gnorm, conv_w, merge_w, layer, tl, chunk):
    bsz, seq, _ = h.shape

    def ts(n):
        return pl.BlockSpec((None, tl, n), lambda b, t: (b, t, 0))

    vec = _layer_spec((1, D_MODEL), layer)
    sq = _layer_spec((D_MODEL, D_MODEL), layer)
    return pl.pallas_call(
        functools.partial(_mixer_seq_body, tl=tl, chunk=chunk),
        out_shape=[jax.ShapeDtypeStruct((bsz, seq, D_MODEL), F32),
                   jax.ShapeDtypeStruct((bsz,) + STATE_TAIL, F32),
                   jax.ShapeDtypeStruct((bsz, CONV_HIST, D_MODEL), F32)],
        grid=(bsz, seq // tl),
        in_specs=[ts(GLA_KEY), ts(GLA_KEY), ts(GLA_KEY), ts(GLA_VAL), ts(GLA_VAL),
                  _const_spec(STATE_TAIL), _layer_spec((1, GLA_VAL), layer),
                  ts(D_MODEL), _const_spec((CONV_HIST, D_MODEL)), ts(D_MODEL), ts(D_MODEL), ts(D_MODEL),
                  _layer_spec((CONV_WIDTH, D_MODEL), layer), vec, vec, sq, sq],
        out_specs=[ts(D_MODEL), pl.BlockSpec((None,) + STATE_TAIL, lambda b, t: (b, 0, 0, 0)),
                   pl.BlockSpec((None, CONV_HIST, D_MODEL), lambda b, t: (b, 0, 0))],
        scratch_shapes=[pltpu.VMEM(STATE_TAIL, F32), pltpu.VMEM((tl, GLA_VAL), F32),
                        pltpu.VMEM((tl + HIST_ROWS, D_MODEL), F32), pltpu.VMEM((tl, D_MODEL), F32)],
        compiler_params=_cparams(2),
        name="mixer_seq",
    )(q, k, la, v, gs, s0, gnorm, u, cache, h, sa, sb, *conv_w, *merge_w)


def kernel(x_prompt, x_sample, state_gla, cache_conv, meta_tokens, norm_ffn1, w_ffn1_gate, w_ffn1_up, w_ffn1_down, norm_mix, w_in, w_decay_up, b_decay, gla_norm, w_dw, b_dw, conv_norm, w_pw, w_out, norm_ffn2, w_ffn2_gate, w_ffn2_up, w_ffn2_down, norm_final):
    n_prompt, l_prompt, _ = x_prompt.shape
    n_sample, l_sample, _ = x_sample.shape
    rows_p = n_prompt * l_prompt
    rows_s = n_sample * l_sample
    rows_small = rows_s + N_META

    hp = x_prompt.reshape(rows_p, D_MODEL)
    hs = jnp.concatenate([x_sample.reshape(rows_s, D_MODEL), meta_tokens], axis=0)

    tm_ffn_p, tm_proj_p, tl_prompt = 512, 512, 512
    tm_proj_s = 208
    sb_gla, sb_conv = 8, 32

    def vec(a):
        return a.reshape(a.shape[0], 1, a.shape[-1])

    def bf(a):
        return a.astype(BF16)

    w_in_t = jnp.swapaxes(w_in, 1, 2)
    w_du = jnp.pad(w_decay_up, ((0, 0), (0, LANE - GATE_RANK), (0, 0))).astype(BF16)

    ffn1_p = (vec(norm_ffn1), w_ffn1_gate, w_ffn1_up, w_ffn1_down)
    ffn2_p = (vec(norm_ffn2), w_ffn2_gate, w_ffn2_up, w_ffn2_down)
    g_mix, b_dec = vec(norm_mix), vec(b_decay)
    conv_w = (w_dw, vec(b_dw), vec(conv_norm))
    merge_w = (bf(w_pw), bf(w_out))
    gnorm = vec(gla_norm)
    gfin = norm_final.reshape(1, D_MODEL)

    zero_state = jnp.zeros((1, 1) + STATE_TAIL, F32)
    zero_cache = jnp.zeros((1, CONV_HIST, 1, D_MODEL), F32)
    cache_tm = jnp.swapaxes(cache_conv, 1, 2)

    def pseq(a):
        return a.reshape(n_prompt, l_prompt, a.shape[-1])

    sp_list, cp_list = [], []
    state_s = cache_s = None
    for l in range(DEPTH):
        last = l == DEPTH - 1
        hp, w1 = _ffn_cast(hp, *ffn1_p, l, tm_ffn_p)
        hs = _ffn(hs, (ffn1_p[0], *w1), l, rows_small)

        (q, k, la, v, gs, u, sa, sb), w_in_l = _proj_cast(hs, g_mix, w_in_t, w_du, b_dec, l, tm_proj_s)
        og_s, state_s = _gla_step(q, k, la, v, gs, state_gla, gnorm, l, l, l_sample, sb_gla, 0, n_sample, state_s)
        og_m, s_m = _gla_step(q, k, la, v, gs, zero_state, gnorm, l, 0, N_META, 1, rows_s, 1, None)
        c_s, cache_s = _conv_step(u, cache_tm, conv_w, l, l, l_sample, sb_conv, 0, n_sample, cache_s)
        c_m, nc_m = _conv_step(u, zero_cache, conv_w, l, 0, N_META, 1, rows_s, 1, None)
        hs = _merge(hs, (c_s, c_m), (og_s, og_m), sa, sb, merge_w, l)

        q, k, la, v, gs, u, sa, sb = _proj(hp, (g_mix, w_in_l, w_du, b_dec), l, tm_proj_p)
        hp, s_p, nc_p = _mixer_seq(pseq(q), pseq(k), pseq(la), pseq(v), pseq(gs), pseq(u), pseq(hp), pseq(sa),
                                   pseq(sb), s_m[0, 0], nc_m[0, :, 0, :], gnorm, conv_w, merge_w, l, tl_prompt, CHUNK)
        hp, w2 = _ffn_cast(hp.reshape(rows_p, D_MODEL), *ffn2_p, l, tm_ffn_p, gfin if last else None)
        hs = _ffn(hs, (ffn2_p[0], *w2), l, rows_small, gfin if last else None)
        sp_list.append(s_p)
        cp_list.append(nc_p)

    y_prompt = hp.reshape(n_prompt, l_prompt, D_MODEL)
    y_sample = hs[:rows_s].reshape(n_sample, l_sample, D_MODEL)
    return (y_prompt, y_sample, jnp.stack(sp_list), jnp.stack(cp_list), state_s, jnp.swapaxes(cache_s, 1, 2))
```

```python
import functools

import jax
import jax.numpy as jnp
from jax import lax
from jax.experimental import pallas as pl
from jax.experimental.pallas import tpu as pltpu

F32 = jnp.float32
BF16 = jnp.bfloat16

D_MODEL = 1024
DEPTH = 4
N_META = 16
GLA_HEADS = 4
GLA_DK = 128
GLA_DV = 256
GLA_KEY = GLA_HEADS * GLA_DK
GLA_VAL = GLA_HEADS * GLA_DV
GATE_RANK = 16
GATE_TAU = 16.0
CHUNK = 64
CONV_WIDTH = 31
CONV_HIST = CONV_WIDTH - 1
D_FF = 2816
EPS = 1e-6

LANE = 128
FF_COLS = 256
HIST_ROWS = 32
CONV_RB = 128
MIX_PARTS = 1
STAGE_SLOTS = 3
STAGE_ROWS_WIDE = 128
STAGE_ROWS_TALL = 512
VMEM_LIMIT = 56 * 1024 * 1024

STATE_TAIL = (GLA_HEADS, GLA_DK, GLA_DV)
NT_DIMS = (((1,), (1,)), ((), ()))
TN_DIMS = (((0,), (0,)), ((), ()))


def _cparams(n_axes):
    return pltpu.CompilerParams(dimension_semantics=("arbitrary",) * n_axes, vmem_limit_bytes=VMEM_LIMIT)


def _const_spec(shape):
    nd = len(shape)
    return pl.BlockSpec(shape, lambda *_: (0,) * nd, pipeline_mode=pl.Buffered(1))


def _layer_spec(shape, layer):
    nd = len(shape)
    return pl.BlockSpec((None,) + tuple(shape), lambda *_: (layer,) + (0,) * nd, pipeline_mode=pl.Buffered(1))


def _alias_prev(in_specs, args, prev_out, out_index):
    if prev_out is None:
        return {}
    in_specs.append(pl.BlockSpec(memory_space=pl.ANY))
    args.append(prev_out)
    return {len(args) - 1: out_index}


def _rmsnorm(x, g):
    return x * lax.rsqrt(jnp.mean(x * x, axis=-1, keepdims=True) + EPS) * g


def _sigmoid(x):
    return 1.0 / (1.0 + jnp.exp(-x))


def _silu(x):
    return x * _sigmoid(x)


def _log_sigmoid(x):
    return jnp.minimum(x, 0.0) - jnp.log1p(jnp.exp(-jnp.abs(x)))


def _dot(a, b):
    return jnp.dot(a, b, preferred_element_type=F32)


def _ffn_math(x, g, wg_ref, wu_ref, wd_ref):
    xn = _rmsnorm(x, g).astype(BF16)
    acts = []
    for c in range(D_FF // FF_COLS):
        cols = slice(c * FF_COLS, (c + 1) * FF_COLS)
        gate = _dot(xn, wg_ref[:, cols])
        up = _dot(xn, wu_ref[:, cols])
        acts.append((_silu(gate) * up).astype(BF16))
    return x + 0.5 * _dot(jnp.concatenate(acts, axis=1), wd_ref[...])


def _ffn_body(h_ref, g_ref, wg_ref, wu_ref, wd_ref, *rest):
    o_ref = rest[-1]
    y = _ffn_math(h_ref[...], g_ref[...], wg_ref, wu_ref, wd_ref)
    if len(rest) == 2:
        y = _rmsnorm(y, rest[0][...])
    o_ref[...] = y


def _ffn(h, w, layer, tm, final_gain=None):
    rows = h.shape[0]
    row_spec = pl.BlockSpec((tm, D_MODEL), lambda i: (i, 0))
    in_specs = [row_spec, _layer_spec((1, D_MODEL), layer), _const_spec((D_MODEL, D_FF)),
                _const_spec((D_MODEL, D_FF)), _const_spec((D_FF, D_MODEL))]
    args = [h, *w]
    if final_gain is not None:
        in_specs.append(_const_spec((1, D_MODEL)))
        args.append(final_gain)
    return pl.pallas_call(
        _ffn_body,
        out_shape=jax.ShapeDtypeStruct((rows, D_MODEL), F32),
        grid=(rows // tm,),
        in_specs=in_specs,
        out_specs=row_spec,
        compiler_params=_cparams(1),
        name="ffn",
    )(*args)


def _ffn_cast_body(h_ref, g_ref, wg_hbm, wu_hbm, wd_hbm, *rest, layer, final):
    (o_ref, wg_out, wu_out, wd_out, wg_s, wu_s, wd_s, stage_a, stage_b, in_sem, out_sem) = rest[-11:]
    i = pl.program_id(0)
    rows_a, rows_b = stage_a.shape[1], stage_b.shape[1]
    n_a, n_b = D_MODEL // rows_a, D_FF // rows_b
    n_chunks = 2 * n_a + n_b

    def chunk(k):
        if k < 2 * n_a:
            hbm, dst = (wg_hbm, wg_s) if k < n_a else (wu_hbm, wu_s)
            return hbm, stage_a, dst, pl.ds((k % n_a) * rows_a, rows_a)
        return wd_hbm, stage_b, wd_s, pl.ds((k - 2 * n_a) * rows_b, rows_b)

    def chunk_copy(k):
        hbm, stage, _, rows = chunk(k)
        slot = k % STAGE_SLOTS
        return pltpu.make_async_copy(hbm.at[layer, rows, :], stage.at[slot], in_sem.at[slot])

    def write_back(j):
        src, dst = ((wg_s, wg_out), (wu_s, wu_out), (wd_s, wd_out))[j]
        return pltpu.make_async_copy(src, dst, out_sem.at[j])

    @pl.when(i == 0)
    def _():
        ahead = STAGE_SLOTS - 1
        for k in range(min(ahead, n_chunks)):
            chunk_copy(k).start()
        for k in range(n_chunks):
            if k + ahead < n_chunks:
                chunk_copy(k + ahead).start()
            chunk_copy(k).wait()
            _, stage, dst, rows = chunk(k)
            dst[rows, :] = stage[k % STAGE_SLOTS].astype(BF16)
        for j in range(3):
            write_back(j).start()

    y = _ffn_math(h_ref[...], g_ref[...], wg_s, wu_s, wd_s)
    if final:
        y = _rmsnorm(y, rest[0][...])
    o_ref[...] = y

    @pl.when(i == pl.num_programs(0) - 1)
    def _():
        for j in range(3):
            write_back(j).wait()


def _ffn_cast(h, g, wg, wu, wd, layer, tm, final_gain=None):
    rows = h.shape[0]
    assert rows // tm >= 2
    row_spec = pl.BlockSpec((tm, D_MODEL), lambda i: (i, 0))
    hbm = pl.BlockSpec(memory_space=pl.ANY)
    in_specs = [row_spec, _layer_spec((1, D_MODEL), layer), hbm, hbm, hbm]
    args = [h, g, wg, wu, wd]
    if final_gain is not None:
        in_specs.append(_const_spec((1, D_MODEL)))
        args.append(final_gain)
    outs = pl.pallas_call(
        functools.partial(_ffn_cast_body, layer=layer, final=final_gain is not None),
        out_shape=[jax.ShapeDtypeStruct((rows, D_MODEL), F32), jax.ShapeDtypeStruct((D_MODEL, D_FF), BF16),
                   jax.ShapeDtypeStruct((D_MODEL, D_FF), BF16), jax.ShapeDtypeStruct((D_FF, D_MODEL), BF16)],
        grid=(rows // tm,),
        in_specs=in_specs,
        out_specs=[row_spec, hbm, hbm, hbm],
        scratch_shapes=[pltpu.VMEM((D_MODEL, D_FF), BF16), pltpu.VMEM((D_MODEL, D_FF), BF16),
                        pltpu.VMEM((D_FF, D_MODEL), BF16),
                        pltpu.VMEM((STAGE_SLOTS, STAGE_ROWS_WIDE, D_FF), F32),
                        pltpu.VMEM((STAGE_SLOTS, FF_COLS, D_MODEL), F32),
                        pltpu.SemaphoreType.DMA((STAGE_SLOTS,)), pltpu.SemaphoreType.DMA((3,))],
        compiler_params=_cparams(1),
        name="ffn_cast",
    )(*args)
    return outs[0], tuple(outs[1:])


_OFF_Q, _OFF_K, _OFF_V, _OFF_G, _OFF_LR, _OFF_A, _OFF_GA, _OFF_MA, _OFF_MB, D_IN = (
    0, 512, 1024, 2048, 3072, 3088, 4112, 5136, 6160, 7184)


def _proj_math(h_ref, g_ref, wt_ref, wdu_ref, bdec_ref,
               q_ref, k_ref, la_ref, v_ref, gs_ref, u_ref, sa_ref, sb_ref):
    xn = _rmsnorm(h_ref[...], g_ref[...]).astype(BF16)

    def mm(lo, hi):
        return lax.dot_general(xn, wt_ref[lo:hi, :], NT_DIMS, preferred_element_type=F32)

    q_ref[...] = mm(_OFF_Q, _OFF_K) * (GLA_DK ** -0.5)
    k_ref[...] = mm(_OFF_K, _OFF_V)
    v_ref[...] = mm(_OFF_V, _OFF_G).astype(BF16)
    gs_ref[...] = _silu(mm(_OFF_G, _OFF_LR)).astype(BF16)
    u_ref[...] = mm(_OFF_A, _OFF_GA) * _sigmoid(mm(_OFF_GA, _OFF_MA))
    sa_ref[...] = _sigmoid(mm(_OFF_MA, _OFF_MB)).astype(BF16)
    sb_ref[...] = _sigmoid(mm(_OFF_MB, D_IN)).astype(BF16)
    lane = lax.broadcasted_iota(jnp.int32, (xn.shape[0], LANE), 1)
    a_lr = jnp.where(lane < GATE_RANK, mm(_OFF_LR, _OFF_LR + LANE), 0.0).astype(BF16)
    z = _dot(a_lr, wdu_ref[...]) + bdec_ref[...]
    la_ref[...] = _log_sigmoid(z) / GATE_TAU


_PROJ_OUTS = ((GLA_KEY, F32), (GLA_KEY, F32), (GLA_KEY, F32), (GLA_VAL, BF16), (GLA_VAL, BF16),
              (D_MODEL, F32), (D_MODEL, BF16), (D_MODEL, BF16))


def _proj(h, w, layer, tm):
    rows = h.shape[0]

    def rs(n):
        return pl.BlockSpec((tm, n), lambda i: (i, 0))

    return pl.pallas_call(
        _proj_math,
        out_shape=[jax.ShapeDtypeStruct((rows, n), dt) for n, dt in _PROJ_OUTS],
        grid=(rows // tm,),
        in_specs=[rs(D_MODEL), _layer_spec((1, D_MODEL), layer), _const_spec((D_IN, D_MODEL)),
                  _layer_spec((LANE, GLA_KEY), layer), _layer_spec((1, GLA_KEY), layer)],
        out_specs=[rs(n) for n, _ in _PROJ_OUTS],
        compiler_params=_cparams(1),
        name="proj",
    )(h, *w)


def _proj_cast_body(h_ref, g_ref, wt_hbm, wdu_ref, bdec_ref, *rest, layer):
    outs = rest[:len(_PROJ_OUTS)]
    wt_out, wt_s, stage, in_sem, out_sem = rest[len(_PROJ_OUTS):]
    i = pl.program_id(0)
    rows_max = stage.shape[1]
    chunks = [(r0, min(rows_max, D_IN - r0)) for r0 in range(0, D_IN, rows_max)]

    def chunk_copy(k):
        r0, n = chunks[k]
        slot = k % STAGE_SLOTS
        return pltpu.make_async_copy(wt_hbm.at[layer, pl.ds(r0, n), :], stage.at[slot, pl.ds(0, n), :],
                                     in_sem.at[slot])

    def write_back():
        return pltpu.make_async_copy(wt_s, wt_out, out_sem.at[0])

    @pl.when(i == 0)
    def _():
        ahead = STAGE_SLOTS - 1
        for k in range(min(ahead, len(chunks))):
            chunk_copy(k).start()
        for k, (r0, n) in enumerate(chunks):
            if k + ahead < len(chunks):
                chunk_copy(k + ahead).start()
            chunk_copy(k).wait()
            wt_s[r0:r0 + n, :] = stage[k % STAGE_SLOTS, 0:n, :].astype(BF16)
        write_back().start()

    _proj_math(h_ref, g_ref, wt_s, wdu_ref, bdec_ref, *outs)

    @pl.when(i == pl.num_programs(0) - 1)
    def _():
        write_back().wait()


def _proj_cast(h, g, wt, wdu, bdec, layer, tm):
    rows = h.shape[0]
    assert rows // tm >= 2

    def rs(n):
        return pl.BlockSpec((tm, n), lambda i: (i, 0))

    hbm = pl.BlockSpec(memory_space=pl.ANY)
    outs = pl.pallas_call(
        functools.partial(_proj_cast_body, layer=layer),
        out_shape=[jax.ShapeDtypeStruct((rows, n), dt) for n, dt in _PROJ_OUTS]
        + [jax.ShapeDtypeStruct((D_IN, D_MODEL), BF16)],
        grid=(rows // tm,),
        in_specs=[rs(D_MODEL), _layer_spec((1, D_MODEL), layer), hbm,
                  _layer_spec((LANE, GLA_KEY), layer), _layer_spec((1, GLA_KEY), layer)],
        out_specs=[rs(n) for n, _ in _PROJ_OUTS] + [hbm],
        scratch_shapes=[pltpu.VMEM((D_IN, D_MODEL), BF16), pltpu.VMEM((STAGE_SLOTS, STAGE_ROWS_TALL, D_MODEL), F32),
                        pltpu.SemaphoreType.DMA((STAGE_SLOTS,)), pltpu.SemaphoreType.DMA((1,))],
        compiler_params=_cparams(1),
        name="proj_cast",
    )(h, g, wt, wdu, bdec)
    return outs[:len(_PROJ_OUTS)], outs[-1]


def _chunk_cumsum(x, chunk):
    pos = lax.broadcasted_iota(jnp.int32, x.shape, 0) % chunk
    shift = 1
    while shift < chunk:
        x = x + jnp.where(pos >= shift, pltpu.roll(x, shift, 0), 0.0)
        shift *= 2
    return x


def _head_norm_gate(o, gnorm, gs):
    return o * lax.rsqrt(jnp.mean(o * o, axis=-1, keepdims=True) + EPS) * gnorm * gs


def _gla_blocks(q, k, v, la, gs, gnorm, states, blk, store):
    n_blk = q.shape[0] // blk
    heads = range(GLA_HEADS)
    kcs = [slice(h * GLA_DK, (h + 1) * GLA_DK) for h in heads]
    vcs = [slice(h * GLA_DV, (h + 1) * GLA_DV) for h in heads]
    carried = not isinstance(states[0], list)
    b_all = _chunk_cumsum(la, blk)
    q_t, k_t, k_u, v_b, dec = [], [], [], [], []
    for j in range(n_blk):
        rows = slice(j * blk, (j + 1) * blk)
        b = b_all[rows, :]
        b_last = b[blk - 1:blk, :]
        kj = k[rows, :]
        q_t.append((q[rows, :] * jnp.exp(b)).astype(BF16))
        k_t.append((kj * jnp.exp(-b)).astype(BF16))
        k_u.append((kj * jnp.exp(b_last - b)).astype(BF16))
        v_b.append(v[rows, :])
        dec.append(jnp.exp(b_last))
    tri = lax.broadcasted_iota(jnp.int32, (blk, blk), 0) >= lax.broadcasted_iota(jnp.int32, (blk, blk), 1)
    eye = (lax.broadcasted_iota(jnp.int32, (GLA_DK, GLA_DK), 0)
           == lax.broadcasted_iota(jnp.int32, (GLA_DK, GLA_DK), 1))
    probs = [[jnp.where(tri, lax.dot_general(q_t[j][:, kcs[h]], k_t[j][:, kcs[h]], NT_DIMS,
                                            preferred_element_type=F32), 0.0).astype(BF16)
              for h in heads] for j in range(n_blk)]
    upd = [[lax.dot_general(k_u[j][:, kcs[h]], v_b[j][:, vcs[h]], TN_DIMS, preferred_element_type=F32)
            for h in heads] for j in range(n_blk)]
    dec_col = [[jnp.sum(jnp.where(eye, jnp.broadcast_to(dec[j][:, kcs[h]], (GLA_DK, GLA_DK)), 0.0),
                        axis=1, keepdims=True) for h in heads] for j in range(n_blk)]
    finals = []
    cur = list(states) if carried else None
    for j in range(n_blk):
        rows = slice(j * blk, (j + 1) * blk)
        if not carried:
            cur = list(states[j])
        for h in heads:
            o = _dot(probs[j][h], v_b[j][:, vcs[h]]) + _dot(q_t[j][:, kcs[h]], cur[h].astype(BF16))
            cur[h] = dec_col[j][h] * cur[h] + upd[j][h]
            store(j, h, _head_norm_gate(o, gnorm[:, vcs[h]], gs[rows, vcs[h]]))
        finals.append(list(cur))
    return finals


def _gla_step_body(q_ref, k_ref, la_ref, v_ref, gs_ref, s0_ref, gn_ref, *rest, sb, seq):
    o_ref, sfin_ref = rest[-2:]

    def store(j, h, o):
        o_ref[j * seq:(j + 1) * seq, h * GLA_DV:(h + 1) * GLA_DV] = o.astype(o_ref.dtype)

    states = [[s0_ref[i, h] for h in range(GLA_HEADS)] for i in range(sb)]
    finals = _gla_blocks(q_ref[...], k_ref[...], v_ref[...], la_ref[...], gs_ref[...].astype(F32), gn_ref[...],
                         states, seq, store)
    for i in range(sb):
        for h in range(GLA_HEADS):
            sfin_ref[i, h] = finals[i][h]


def _gla_step(q, k, la, v, gs, s0_all, gnorm, layer, state_layer, seq, sb, row0, n_seq, prev_state):
    blk = sb * seq
    base = row0 // blk

    def bs(n, first=base):
        return pl.BlockSpec((blk, n), lambda i: (first + i, 0))

    st = pl.BlockSpec((None, sb) + STATE_TAIL, lambda i: (state_layer, i, 0, 0, 0))
    in_specs = [bs(GLA_KEY), bs(GLA_KEY), bs(GLA_KEY), bs(GLA_VAL), bs(GLA_VAL), st, _layer_spec((1, GLA_VAL), layer)]
    args = [q, k, la, v, gs, s0_all, gnorm]
    aliases = _alias_prev(in_specs, args, prev_state, 1)
    return pl.pallas_call(
        functools.partial(_gla_step_body, sb=sb, seq=seq),
        out_shape=[jax.ShapeDtypeStruct((n_seq * seq, GLA_VAL), F32), jax.ShapeDtypeStruct(s0_all.shape, F32)],
        grid=(n_seq // sb,),
        in_specs=in_specs,
        out_specs=[bs(GLA_VAL, 0), st],
        input_output_aliases=aliases,
        compiler_params=_cparams(1),
        name="gla_step",
    )(*args)


def _conv_act(acc, cn):
    return _silu(_rmsnorm(acc, cn))


def _conv_step_body(u_ref, cache_ref, wdw_ref, bdw_ref, cn_ref, *rest, sb, seq):
    c_ref, ncache_ref = rest[-2:]
    u_t = jnp.swapaxes(u_ref[...].reshape(sb, seq, D_MODEL), 0, 1)

    def up(m):
        return cache_ref[m] if m < CONV_HIST else u_t[m - CONV_HIST]

    outs = []
    for t in range(seq):
        acc = jnp.broadcast_to(bdw_ref[...], (sb, D_MODEL))
        for j in range(CONV_WIDTH):
            acc = acc + wdw_ref[j:j + 1, :] * up(t + j)
        outs.append(_conv_act(acc, cn_ref[...]))
    c_ref[...] = jnp.swapaxes(jnp.stack(outs), 0, 1).reshape(sb * seq, D_MODEL)
    for m in range(CONV_HIST):
        ncache_ref[m] = up(m + seq)


def _conv_step(u, cache_all, conv_w, layer, cache_layer, seq, sb, row0, n_seq, prev_cache):
    blk = sb * seq
    base = row0 // blk
    us = pl.BlockSpec((blk, D_MODEL), lambda i: (base + i, 0))
    cs = pl.BlockSpec((None, CONV_HIST, sb, D_MODEL), lambda i: (cache_layer, 0, i, 0))
    in_specs = [us, cs, _layer_spec((CONV_WIDTH, D_MODEL), layer), _layer_spec((1, D_MODEL), layer),
                _layer_spec((1, D_MODEL), layer)]
    args = [u, cache_all, *conv_w]
    aliases = _alias_prev(in_specs, args, prev_cache, 1)
    return pl.pallas_call(
        functools.partial(_conv_step_body, sb=sb, seq=seq),
        out_shape=[jax.ShapeDtypeStruct((n_seq * seq, D_MODEL), F32), jax.ShapeDtypeStruct(cache_all.shape, F32)],
        grid=(n_seq // sb,),
        in_specs=in_specs,
        out_specs=[pl.BlockSpec((blk, D_MODEL), lambda i: (i, 0)), cs],
        input_output_aliases=aliases,
        compiler_params=_cparams(1),
        name="conv_step",
    )(*args)


def _conv_piece(ubuf, wdw_ref, r0, l0):
    pad = HIST_ROWS - CONV_HIST
    xs = ubuf[r0:r0 + CONV_RB + HIST_ROWS, l0:l0 + LANE]
    out = None
    for s in range(8):
        rows = CONV_RB if s == 0 else CONV_RB + 8
        z = None
        for a in range(HIST_ROWS // 8 + 1):
            j = 8 * a + s - pad
            if 0 <= j < CONV_WIDTH:
                term = wdw_ref[j:j + 1, l0:l0 + LANE] * xs[8 * a:8 * a + rows]
                z = term if z is None else z + term
        z = z[s:s + CONV_RB]
        out = z if out is None else out + z
    return out


def _merge_math(h, c_act, og, sa, sb, wpw_ref, wout_ref):
    o_conv = _dot(c_act, wpw_ref[...])
    merged = (sa * og + sb * o_conv).astype(BF16)
    return h + _dot(merged, wout_ref[...])


def _merge_body(h_ref, ca_ref, cb_ref, oga_ref, ogb_ref, sa_ref, sb_ref, wpw_ref, wout_ref, o_ref):
    c_act = jnp.concatenate([ca_ref[...], cb_ref[...]], axis=0).astype(BF16)
    og = jnp.concatenate([oga_ref[...], ogb_ref[...]], axis=0)
    o_ref[...] = _merge_math(h_ref[...], c_act, og, sa_ref[...].astype(F32), sb_ref[...].astype(F32),
                             wpw_ref, wout_ref)


def _merge(h, c_parts, og_parts, sa, sb, merge_w, layer):
    rows = h.shape[0]

    def whole(a):
        return _const_spec(a.shape)

    sq = _layer_spec((D_MODEL, D_MODEL), layer)
    args = [h, *c_parts, *og_parts, sa, sb]
    return pl.pallas_call(
        _merge_body,
        out_shape=jax.ShapeDtypeStruct((rows, D_MODEL), F32),
        grid=(1,),
        in_specs=[whole(a) for a in args] + [sq, sq],
        out_specs=pl.BlockSpec((rows, D_MODEL), lambda i: (0, 0)),
        compiler_params=_cparams(1),
        name="merge",
    )(*args, *merge_w)


def _mixer_seq_body(q_ref, k_ref, la_ref, v_ref, gs_ref, s0_ref, gn_ref, u_ref, cache_ref, h_ref, sa_ref, sb_ref,
                    wdw_ref, bdw_ref, cn_ref, wpw_ref, wout_ref, o_ref, sfin_ref, ncache_ref,
                    s_scr, og_scr, ubuf, cbuf, *, tl, chunk):
    t = pl.program_id(1)
    pad = HIST_ROWS - CONV_HIST

    @pl.when(t == 0)
    def _():
        s_scr[...] = s0_ref[...]
        ubuf[0:pad, :] = jnp.zeros((pad, D_MODEL), F32)
        ubuf[pad:HIST_ROWS, :] = cache_ref[...]

    @pl.when(t > 0)
    def _():
        ubuf[0:HIST_ROWS, :] = ubuf[tl:tl + HIST_ROWS, :]

    def store(j, h, o):
        og_scr[j * chunk:(j + 1) * chunk, h * GLA_DV:(h + 1) * GLA_DV] = o

    finals = _gla_blocks(q_ref[...], k_ref[...], v_ref[...], la_ref[...], gs_ref[...].astype(F32), gn_ref[...],
                         [s_scr[h] for h in range(GLA_HEADS)], chunk, store)
    for h in range(GLA_HEADS):
        s_scr[h] = finals[-1][h]

    ubuf[HIST_ROWS:HIST_ROWS + tl, :] = u_ref[...]
    part = tl // MIX_PARTS
    for p in range(MIX_PARTS):
        prow = slice(p * part, (p + 1) * part)
        for rb in range(p * part // CONV_RB, (p + 1) * part // CONV_RB):
            rows = slice(rb * CONV_RB, (rb + 1) * CONV_RB)
            for lc in range(D_MODEL // LANE):
                lanes = slice(lc * LANE, (lc + 1) * LANE)
                cbuf[rows, lanes] = _conv_piece(ubuf, wdw_ref, rb * CONV_RB, lc * LANE) + bdw_ref[:, lanes]
        c_act = _conv_act(cbuf[prow, :], cn_ref[...]).astype(BF16)
        o_ref[prow, :] = _merge_math(h_ref[prow, :], c_act, og_scr[prow, :], sa_ref[prow, :].astype(F32),
                                     sb_ref[prow, :].astype(F32), wpw_ref, wout_ref)

    @pl.when(t == pl.num_programs(1) - 1)
    def _():
        sfin_ref[...] = s_scr[...]
        ncache_ref[...] = ubuf[tl + pad:tl + HIST_ROWS, :]


def _mixer_seq(q, k, la, v, gs, u, h, sa, sb, s0, cache, gnorm, conv_w, merge_w, layer, tl, chunk):
    bsz, seq, _ = h.shape

    def ts(n):
        return pl.BlockSpec((None, tl, n), lambda b, t: (b, t, 0))

    vec = _layer_spec((1, D_MODEL), layer)
    sq = _layer_spec((D_MODEL, D_MODEL), layer)
    return pl.pallas_call(
        functools.partial(_mixer_seq_body, tl=tl, chunk=chunk),
        out_shape=[jax.ShapeDtypeStruct((bsz, seq, D_MODEL), F32),
                   jax.ShapeDtypeStruct((bsz,) + STATE_TAIL, F32),
                   jax.ShapeDtypeStruct((bsz, CONV_HIST, D_MODEL), F32)],
        grid=(bsz, seq // tl),
        in_specs=[ts(GLA_KEY), ts(GLA_KEY), ts(GLA_KEY), ts(GLA_VAL), ts(GLA_VAL),
                  _const_spec(STATE_TAIL), _layer_spec((1, GLA_VAL), layer),
                  ts(D_MODEL), _const_spec((CONV_HIST, D_MODEL)), ts(D_MODEL), ts(D_MODEL), ts(D_MODEL),
                  _layer_spec((CONV_WIDTH, D_MODEL), layer), vec, vec, sq, sq],
        out_specs=[ts(D_MODEL), pl.BlockSpec((None,) + STATE_TAIL, lambda b, t: (b, 0, 0, 0)),
                   pl.BlockSpec((None, CONV_HIST, D_MODEL), lambda b, t: (b, 0, 0))],
        scratch_shapes=[pltpu.VMEM(STATE_TAIL, F32), pltpu.VMEM((tl, GLA_VAL), F32),
                        pltpu.VMEM((tl + HIST_ROWS, D_MODEL), F32), pltpu.VMEM((tl, D_MODEL), F32)],
        compiler_params=_cparams(2),
        name="mixer_seq",
    )(q, k, la, v, gs, s0, gnorm, u, cache, h, sa, sb, *conv_w, *merge_w)


def kernel(x_prompt, x_sample, state_gla, cache_conv, meta_tokens, norm_ffn1, w_ffn1_gate, w_ffn1_up, w_ffn1_down, norm_mix, w_in, w_decay_up, b_decay, gla_norm, w_dw, b_dw, conv_norm, w_pw, w_out, norm_ffn2, w_ffn2_gate, w_ffn2_up, w_ffn2_down, norm_final):
    n_prompt, l_prompt, _ = x_prompt.shape
    n_sample, l_sample, _ = x_sample.shape
    rows_p = n_prompt * l_prompt
    rows_s = n_sample * l_sample
    rows_small = rows_s + N_META

    hp = x_prompt.reshape(rows_p, D_MODEL)
    hs = jnp.concatenate([x_sample.reshape(rows_s, D_MODEL), meta_tokens], axis=0)

    tm_ffn_p, tm_proj_p, tl_prompt = 512, 512, 512
    tm_proj_s = 208
    sb_gla, sb_conv = 8, 32

    def vec(a):
        return a.reshape(a.shape[0], 1, a.shape[-1])

    def bf(a):
        return a.astype(BF16)

    w_in_t = jnp.swapaxes(w_in, 1, 2)
    w_du = jnp.pad(w_decay_up, ((0, 0), (0, LANE - GATE_RANK), (0, 0))).astype(BF16)

    ffn1_p = (vec(norm_ffn1), w_ffn1_gate, w_ffn1_up, w_ffn1_down)
    ffn2_p = (vec(norm_ffn2), w_ffn2_gate, w_ffn2_up, w_ffn2_down)
    g_mix, b_dec = vec(norm_mix), vec(b_decay)
    conv_w = (w_dw, vec(b_dw), vec(conv_norm))
    merge_w = (bf(w_pw), bf(w_out))
    gnorm = vec(gla_norm)
    gfin = norm_final.reshape(1, D_MODEL)

    zero_state = jnp.zeros((1, 1) + STATE_TAIL, F32)
    zero_cache = jnp.zeros((1, CONV_HIST, 1, D_MODEL), F32)
    cache_tm = jnp.swapaxes(cache_conv, 1, 2)

    def pseq(a):
        return a.reshape(n_prompt, l_prompt, a.shape[-1])

    sp_list, cp_list = [], []
    state_s = cache_s = None
    for l in range(DEPTH):
        last = l == DEPTH - 1
        hp, w1 = _ffn_cast(hp, *ffn1_p, l, tm_ffn_p)
        hs = _ffn(hs, (ffn1_p[0], *w1), l, rows_small)

        (q, k, la, v, gs, u, sa, sb), w_in_l = _proj_cast(hs, g_mix, w_in_t, w_du, b_dec, l, tm_proj_s)
        og_s, state_s = _gla_step(q, k, la, v, gs, state_gla, gnorm, l, l, l_sample, sb_gla, 0, n_sample, state_s)
        og_m, s_m = _gla_step(q, k, la, v, gs, zero_state, gnorm, l, 0, N_META, 1, rows_s, 1, None)
        c_s, cache_s = _conv_step(u, cache_tm, conv_w, l, l, l_sample, sb_conv, 0, n_sample, cache_s)
        c_m, nc_m = _conv_step(u, zero_cache, conv_w, l, 0, N_META, 1, rows_s, 1, None)
        hs = _merge(hs, (c_s, c_m), (og_s, og_m), sa, sb, merge_w, l)

        q, k, la, v, gs, u, sa, sb = _proj(hp, (g_mix, w_in_l, w_du, b_dec), l, tm_proj_p)
        hp, s_p, nc_p = _mixer_seq(pseq(q), pseq(k), pseq(la), pseq(v), pseq(gs), pseq(u), pseq(hp), pseq(sa),
                                   pseq(sb), s_m[0, 0], nc_m[0, :, 0, :], gnorm, conv_w, merge_w, l, tl_prompt, CHUNK)
        hp, w2 = _ffn_cast(hp.reshape(rows_p, D_MODEL), *ffn2_p, l, tm_ffn_p, gfin if last else None)
        hs = _ffn(hs, (ffn2_p[0], *w2), l, rows_small, gfin if last else None)
        sp_list.append(s_p)
        cp_list.append(nc_p)

    y_prompt = hp.reshape(n_prompt, l_prompt, D_MODEL)
    y_sample = hs[:rows_s].reshape(n_sample, l_sample, D_MODEL)
    return (y_prompt, y_sample, jnp.stack(sp_list), jnp.stack(cp_list), state_s, jnp.swapaxes(cache_s, 1, 2))
```
